```python
import jax, jax.numpy as jnp
from jax import lax
import numpy as np

D_MODEL = 1024
BATCH = 16
SEQ = 256
DEPTH = 2
DEC_BATCH = 8
DEC_SEQ = 4096
PAST_LEN = 256

GRID_W = 64
NA_HEADS = 8
NA_HEAD_DIM = 64
NA_WIDTH = NA_HEADS * NA_HEAD_DIM
NA_KH = 8
NA_KW = 16
NA_QB = 16
NA_KB = NA_QB + NA_KW
NA_SCALE = NA_HEAD_DIM ** -0.5
Q_BLOCK = 128
M_HEADS = 4
M_DK = 128
M_DV = 128
M_WIDTH = M_HEADS * M_DV
M_CHUNK = 128
M_SCALE = M_DK ** -0.5
M_F_BIAS_LO = 3.0
M_F_BIAS_HI = 6.0
CONV_WIDTH = 512
CONV_K = 3
D_FF = ((8 * D_MODEL // 3 + 255) // 256) * 256
N_BRANCH = 3
N_MOD = 6
EPS = 1e-6
NEG = -1e30

SPLITS = ([NA_WIDTH] * 3
          + [M_HEADS * M_DK, M_HEADS * M_DK, M_WIDTH, M_WIDTH, 2 * M_HEADS, 2 * M_HEADS]
          + [CONV_WIDTH] * 3
          + [N_BRANCH * D_MODEL])
D_IN = sum(SPLITS)
SPLIT_IDX = np.cumsum(SPLITS)[:-1].tolist()

kernel_name = 'hybrid_na_mlstm_conv_diffusion_step'


def rmsnorm(x, g):
    xf = x.astype(jnp.float32)
    y = xf * lax.rsqrt(jnp.mean(xf * xf, axis=-1, keepdims=True) + EPS)
    return (y * g.astype(jnp.float32)).astype(x.dtype)


def split_heads(a, n):
    b, t, _ = a.shape
    return a.reshape(b, t, n, -1).transpose(0, 2, 1, 3)


def merge_heads(a):
    b, h, t, d = a.shape
    return a.transpose(0, 2, 1, 3).reshape(b, t, h * d)


def context_attention(q, k, v):
    B, H, S, dh = q.shape
    qb = jnp.moveaxis(q.reshape(B, H, S // Q_BLOCK, Q_BLOCK, dh), 2, 0)

    def block(qi):
        s = jnp.einsum('bhqd,bhkd->bhqk', qi, k).astype(jnp.float32)
        p = jax.nn.softmax(s, axis=-1).astype(v.dtype)
        return jnp.einsum('bhqk,bhkd->bhqd', p, v)

    o = lax.map(block, qb)
    return jnp.moveaxis(o, 0, 2).reshape(B, H, S, dh)


def neighborhood_attention(q, k, v, k_ctx, v_ctx, rpb):
    B, H, N, dh = q.shape
    rows = N // GRID_W
    kh = min(NA_KH, rows)
    nqb = GRID_W // NA_QB
    qcols = np.arange(GRID_W).reshape(nqb, NA_QB)
    kstart = np.clip(qcols[:, 0] - NA_KW // 2, 0, GRID_W - NA_KB)
    kcols = kstart[:, None] + np.arange(NA_KB)
    cs = np.clip(qcols - NA_KW // 2, 0, GRID_W - NA_KW)
    kk = kcols[:, None, :]
    col_mask = jnp.asarray(np.where((kk >= cs[:, :, None]) & (kk < cs[:, :, None] + NA_KW), 0.0, NEG),
                           dtype=jnp.float32)
    dc_idx = np.clip(kk - qcols[:, :, None] + NA_KW - 1, 0, 2 * NA_KW - 2)
    bias_c = rpb.astype(jnp.float32)[:, :, dc_idx]
    kg = k.reshape(B, H, rows, GRID_W, dh)[:, :, :, kcols]
    vg = v.reshape(B, H, rows, GRID_W, dh)[:, :, :, kcols]
    qg = jnp.moveaxis(q.reshape(B, H, rows, nqb, NA_QB, dh), 2, 0)
    n_loc = kh * NA_KB

    def row_block(args):
        r, qr = args
        rs = jnp.clip(r - kh // 2, 0, rows - kh)
        kr = lax.dynamic_slice_in_dim(kg, rs, kh, axis=2)
        vr = lax.dynamic_slice_in_dim(vg, rs, kh, axis=2)
        bias = jnp.take(bias_c, rs + jnp.arange(kh) - r + NA_KH - 1, axis=1).transpose(0, 2, 3, 1, 4)
        s_loc = (jnp.einsum('bhjqd,bhijkd->bhjqik', qr, kr).astype(jnp.float32)
                 + bias[None] + col_mask[None, None, :, :, None, :])
        s_ctx = jnp.einsum('bhjqd,bhpd->bhjqp', qr, k_ctx).astype(jnp.float32)
        p = jax.nn.softmax(jnp.concatenate([s_loc.reshape(B, H, nqb, NA_QB, n_loc), s_ctx], axis=-1),
                           axis=-1).astype(v.dtype)
        p_loc = p[..., :n_loc].reshape(B, H, nqb, NA_QB, kh, NA_KB)
        return (jnp.einsum('bhjqik,bhijkd->bhjqd', p_loc, vr)
                + jnp.einsum('bhjqp,bhpd->bhjqd', p[..., n_loc:], v_ctx))

    o = lax.map(row_block, (jnp.arange(rows), qg))
    return jnp.moveaxis(o, 0, 2).reshape(B, H, N, dh)


def mlstm_chunkwise(q, k, v, i_pre, log_f, C0, n0, m0):
    B, H, N, _ = q.shape
    dv = v.shape[-1]
    nc = N // M_CHUNK
    causal = jnp.asarray(np.tril(np.ones((M_CHUNK, M_CHUNK), dtype=bool)))

    def chunks(a):
        return jnp.moveaxis(a.reshape(a.shape[:2] + (nc, M_CHUNK) + a.shape[3:]), 2, 0)

    def step(carry, inp):
        C, n, m = carry
        qc, kc, vc, ic, fc = inp
        b = jnp.cumsum(fc, axis=-1)
        d = jnp.where(causal, b[..., :, None] - b[..., None, :] + ic[..., None, :], NEG)
        inter = b + m[..., None]
        mt = jnp.maximum(inter, jnp.max(d, axis=-1))
        w_int = jnp.exp(inter - mt)
        s = jnp.einsum('bhtd,bhsd->bhts', qc, kc) * jnp.exp(d - mt[..., None])
        num = w_int[..., None] * jnp.einsum('bhtd,bhde->bhte', qc, C) + jnp.einsum('bhts,bhse->bhte', s, vc)
        den = w_int * jnp.einsum('bhtd,bhd->bht', qc, n) + jnp.sum(s, axis=-1)
        h = num / jnp.maximum(jnp.abs(den), jnp.exp(-mt))[..., None]
        g = b[..., -1:] - b + ic
        m_new = jnp.maximum(b[..., -1] + m, jnp.max(g, axis=-1))
        wk = jnp.exp(g - m_new[..., None])
        decay = jnp.exp(b[..., -1] + m - m_new)
        C_new = decay[..., None, None] * C + jnp.einsum('bhs,bhsd,bhse->bhde', wk, kc, vc)
        n_new = decay[..., None] * n + jnp.einsum('bhs,bhsd->bhd', wk, kc)
        return (C_new, n_new, m_new), h

    xs = tuple(chunks(a) for a in (q, k, v, i_pre, log_f))
    (C, n, m), hs = lax.scan(step, (C0, n0, m0), xs)
    h = jnp.moveaxis(hs, 0, 2).reshape(B, H, N, dv)
    return h, C, n, m


def bidir_mlstm(q, k, v, i_pre, f_pre, C0, n0, m0):
    f32 = jnp.float32
    q, k, v, i_pre = q.astype(f32), k.astype(f32), v.astype(f32), i_pre.astype(f32)
    log_f = jax.nn.log_sigmoid(f_pre.astype(f32))
    C0, n0, m0 = C0.astype(f32), n0.astype(f32), m0.astype(f32)
    h_f, Cf, nf, mf = mlstm_chunkwise(q, k, v, i_pre[:, 0], log_f[:, 0], C0[:, 0], n0[:, 0], m0[:, 0])
    rev = lambda a: jnp.flip(a, axis=2)
    h_b, Cb, nb, mb = mlstm_chunkwise(rev(q), rev(k), rev(v), rev(i_pre[:, 1]), rev(log_f[:, 1]),
                                      C0[:, 1], n0[:, 1], m0[:, 1])
    h = h_f + rev(h_b)
    return h, jnp.stack([Cf, Cb], axis=1), jnp.stack([nf, nb], axis=1), jnp.stack([mf, mb], axis=1)


def short_conv(u, w):
    up = jnp.pad(u, ((0, 0), (1, 1), (0, 0)))
    return up[:, :-2] * w[0] + up[:, 1:-1] * w[1] + up[:, 2:] * w[2]


def trunk_layer(x, cond, P, l, ctx):
    B, N, _ = x.shape
    mod = (jax.nn.silu(cond) @ P['w_ada'][l] + P['b_ada'][l]).reshape(cond.shape[0], N_MOD, D_MODEL)
    shift1, scale1, gate1, shift2, scale2, gate2 = (mod[:, i, None, :] for i in range(N_MOD))

    h = rmsnorm(x, P['g_pre_mix'][l]) * (1 + scale1) + shift1
    (na_q, na_k, na_v, m_q, m_k, m_v, m_o, m_i, m_f, cv_b, cv_c, cv_x, gate_pre) = jnp.split(
        h @ P['w_in'][l], SPLIT_IDX, axis=-1)

    q = split_heads(na_q, NA_HEADS) * NA_SCALE
    k = split_heads(na_k, NA_HEADS)
    v = split_heads(na_v, NA_HEADS)
    if ctx is None:
        att = context_attention(q, k, v)
        C0 = jnp.zeros((B, 2, M_HEADS, M_DK, M_DV), jnp.float32)
        n0 = jnp.zeros((B, 2, M_HEADS, M_DK), jnp.float32)
        m0 = jnp.zeros((B, 2, M_HEADS), jnp.float32)
    else:
        k_ctx, v_ctx, C0, n0, m0 = ctx
        att = neighborhood_attention(q, k, v, k_ctx, v_ctx, P['na_rpb'][l])

    i_pre = (m_i.reshape(B, N, 2, M_HEADS) + P['m_b_i'][l]).transpose(0, 2, 3, 1)
    f_pre = (m_f.reshape(B, N, 2, M_HEADS) + P['m_b_f'][l]).transpose(0, 2, 3, 1)
    hm, C, n, m = bidir_mlstm(split_heads(m_q, M_HEADS) * M_SCALE, split_heads(m_k, M_HEADS),
                              split_heads(m_v, M_HEADS), i_pre, f_pre, C0, n0, m0)
    hm = hm * lax.rsqrt(jnp.mean(hm * hm, axis=-1, keepdims=True) + EPS)
    hm = merge_heads(hm).astype(x.dtype) * P['m_norm_g'][l] * jax.nn.sigmoid(m_o)

    hc = cv_b * short_conv(cv_c * cv_x, P['conv_w'][l])

    g = jax.nn.sigmoid(gate_pre + P['b_gate'][l]).reshape(B, N, N_BRANCH, D_MODEL)
    merged = (g[:, :, 0] * (merge_heads(att) @ P['w_br_na'][l])
              + g[:, :, 1] * (hm @ P['w_br_m'][l])
              + g[:, :, 2] * (hc @ P['w_br_cv'][l]))
    x = x + gate1 * rmsnorm(merged @ P['w_out'][l], P['g_post_mix'][l])

    h2 = rmsnorm(x, P['g_pre_ffn'][l]) * (1 + scale2) + shift2
    ff = (jax.nn.silu(h2 @ P['w_ffn_gate'][l]) * (h2 @ P['w_ffn_up'][l])) @ P['w_ffn_down'][l]
    x = x + gate2 * rmsnorm(ff, P['g_post_ffn'][l])
    new_ctx = (k, v, C, n, m) if ctx is None else None
    return x, new_ctx


def setup_inputs(seed: int = 0) -> dict:
    key = jax.random.key(seed)
    ks = jax.random.split(key, 32)
    nrm = lambda i, shape, s=1.0: jax.random.normal(ks[i], shape, jnp.float32) * s
    D = D_MODEL
    return {
        'x_prompt': nrm(0, (BATCH, SEQ, D)),
        'x_sample': nrm(1, (DEC_BATCH, DEC_SEQ, D)),
        'c': nrm(2, (DEC_BATCH, D)),
        'cache_k': nrm(3, (DEC_BATCH, DEPTH, NA_HEADS, PAST_LEN, NA_HEAD_DIM)),
        'cache_v': nrm(4, (DEC_BATCH, DEPTH, NA_HEADS, PAST_LEN, NA_HEAD_DIM)),
        'state_C': nrm(5, (DEC_BATCH, DEPTH, 2, M_HEADS, M_DK, M_DV), 0.5),
        'state_n': nrm(6, (DEC_BATCH, DEPTH, 2, M_HEADS, M_DK), 0.5),
        'state_m': jnp.abs(nrm(7, (DEC_BATCH, DEPTH, 2, M_HEADS), 0.5)),
        'c_ctx': nrm(8, (D,)),
        'w_ada': nrm(9, (DEPTH, D, N_MOD * D), D ** -0.5),
        'b_ada': nrm(10, (DEPTH, N_MOD * D), 0.02),
        'g_pre_mix': 1.0 + nrm(11, (DEPTH, D), 0.02),
        'g_post_mix': 1.0 + nrm(12, (DEPTH, D), 0.02),
        'g_pre_ffn': 1.0 + nrm(13, (DEPTH, D), 0.02),
        'g_post_ffn': 1.0 + nrm(14, (DEPTH, D), 0.02),
        'w_in': nrm(15, (DEPTH, D, D_IN), D ** -0.5),
        'b_gate': nrm(16, (DEPTH, N_BRANCH * D), 0.02),
        'm_b_i': nrm(17, (DEPTH, 2, M_HEADS), 0.1),
        'm_b_f': jnp.linspace(M_F_BIAS_LO, M_F_BIAS_HI, M_HEADS, dtype=jnp.float32) + nrm(18, (DEPTH, 2, M_HEADS), 0.1),
        'na_rpb': nrm(19, (DEPTH, NA_HEADS, 2 * NA_KH - 1, 2 * NA_KW - 1), 0.1),
        'm_norm_g': 1.0 + nrm(20, (DEPTH, M_WIDTH), 0.02),
        'conv_w': nrm(21, (DEPTH, CONV_K, CONV_WIDTH), CONV_K ** -0.5),
        'w_br_na': nrm(22, (DEPTH, NA_WIDTH, D), NA_WIDTH ** -0.5),
        'w_br_m': nrm(23, (DEPTH, M_WIDTH, D), M_WIDTH ** -0.5),
        'w_br_cv': nrm(24, (DEPTH, CONV_WIDTH, D), CONV_WIDTH ** -0.5),
        'w_out': nrm(25, (DEPTH, D, D), D ** -0.5),
        'w_ffn_gate': nrm(26, (DEPTH, D, D_FF), D ** -0.5),
        'w_ffn_up': nrm(27, (DEPTH, D, D_FF), D ** -0.5),
        'w_ffn_down': nrm(28, (DEPTH, D_FF, D), D_FF ** -0.5),
    }


def reference(x_prompt, x_sample, c, cache_k, cache_v, state_C, state_n, state_m, c_ctx,
              w_ada, b_ada, g_pre_mix, g_post_mix, g_pre_ffn, g_post_ffn, w_in, b_gate,
              m_b_i, m_b_f, na_rpb, m_norm_g, conv_w, w_br_na, w_br_m, w_br_cv, w_out,
              w_ffn_gate, w_ffn_up, w_ffn_down):
    P = dict(w_ada=w_ada, b_ada=b_ada, g_pre_mix=g_pre_mix, g_post_mix=g_post_mix,
             g_pre_ffn=g_pre_ffn, g_post_ffn=g_post_ffn, w_in=w_in, b_gate=b_gate,
             m_b_i=m_b_i, m_b_f=m_b_f, na_rpb=na_rpb, m_norm_g=m_norm_g, conv_w=conv_w,
             w_br_na=w_br_na, w_br_m=w_br_m, w_br_cv=w_br_cv, w_out=w_out,
             w_ffn_gate=w_ffn_gate, w_ffn_up=w_ffn_up, w_ffn_down=w_ffn_down)
    xp, xs = x_prompt, x_sample
    ks_, vs_, Cs_, ns_, ms_ = [], [], [], [], []
    for l in range(DEPTH):
        xp, (k_l, v_l, C_l, n_l, m_l) = trunk_layer(xp, c_ctx[None, :], P, l, None)
        ks_.append(k_l); vs_.append(v_l); Cs_.append(C_l); ns_.append(n_l); ms_.append(m_l)
        xs, _ = trunk_layer(xs, c, P, l, (cache_k[:, l], cache_v[:, l], state_C[:, l],
                                          state_n[:, l], state_m[:, l]))
    new_k = jnp.stack(ks_, axis=1)
    new_v = jnp.stack(vs_, axis=1)
    new_C = jnp.stack(Cs_, axis=1)
    new_n = jnp.stack(ns_, axis=1)
    new_m = jnp.stack(ms_, axis=1)
    return (xp, xs, new_k, new_v, new_C, new_n, new_m)
```

```python
import functools

import numpy as np
import jax
import jax.numpy as jnp
from jax import lax
from jax.experimental import pallas as pl
from jax.experimental.pallas import tpu as pltpu

F32 = jnp.float32
BF16 = jnp.bfloat16

GRID_W = 64
NA_HEADS = 8
NA_HEAD_DIM = 64
NA_WIDTH = NA_HEADS * NA_HEAD_DIM
NA_KH = 8
NA_KW = 16
NA_SCALE = NA_HEAD_DIM ** -0.5
M_HEADS = 4
M_DK = 128
M_DV = 128
M_WIDTH = M_HEADS * M_DV
M_CHUNK = 128
M_SCALE = M_DK ** -0.5
CONV_WIDTH = 512
N_BRANCH = 3
N_MOD = 6
EPS = 1e-6
NEG = -1e30

LANES = 128
SUBLANES = 8
VMEM_LIMIT_BYTES = 56 * 1024 * 1024

TOKEN_TILE = 512
NA_ROWS_PER_BLOCK = 4
FFN_CHUNKS = 2
ADA_TILE = 1024
COND_ROWS = 16


def _const_spec(shape):
    nd = len(shape)
    return pl.BlockSpec(shape, lambda *_: (0,) * nd, pipeline_mode=pl.Buffered(1))


def _params(*sem):
    return pltpu.CompilerParams(dimension_semantics=sem, vmem_limit_bytes=VMEM_LIMIT_BYTES)


def _sigmoid(x):
    return 1.0 / (1.0 + jnp.exp(-x))


def _log_sigmoid(x):
    return jnp.minimum(x, 0.0) - jnp.log1p(jnp.exp(-jnp.abs(x)))


def _rmsnorm(x, g):
    return x * lax.rsqrt(jnp.mean(x * x, axis=-1, keepdims=True) + EPS) * g


def _dot(a, b):
    return jnp.dot(a, b, preferred_element_type=F32)


def _dot_nt(a, b):
    return lax.dot_general(a, b, (((1,), (1,)), ((), ())), preferred_element_type=F32)


def _dot_tn(a, b):
    return lax.dot_general(a, b, (((0,), (0,)), ((), ())), preferred_element_type=F32)


def _ada_kernel(cond_ref, w_ref, b_ref, o_ref):
    c = cond_ref[...]
    s = (c * _sigmoid(c)).astype(BF16)
    o_ref[0] = _dot(s, w_ref[0].astype(BF16)) + b_ref[0]


def _ada_modulation(cond, w_ada, b_ada):
    depth, d, n = w_ada.shape
    return pl.pallas_call(
        _ada_kernel,
        grid=(depth, n // ADA_TILE),
        in_specs=[
            pl.BlockSpec((COND_ROWS, d), lambda l, j: (0, 0)),
            pl.BlockSpec((1, d, ADA_TILE), lambda l, j: (l, 0, j)),
            pl.BlockSpec((1, 1, ADA_TILE), lambda l, j: (l, 0, j)),
        ],
        out_specs=pl.BlockSpec((1, COND_ROWS, ADA_TILE), lambda l, j: (l, 0, j)),
        out_shape=jax.ShapeDtypeStruct((depth, COND_ROWS, n), F32),
        compiler_params=_params("arbitrary", "arbitrary"),
        name="ada_modulation",
    )(cond, w_ada, b_ada.reshape(depth, 1, n))


def _mod_index_map(per_seq, tiles_per_seq):
    if per_seq:
        return lambda i: (i // tiles_per_seq, 0, 0)
    return lambda i: (0, 0, 0)


def _inproj_kernel(x_ref, mod_ref, g_ref, wa_ref, wif_ref, bif_ref, wcv_ref,
                   naqkv_ref, mqkv_ref, if_ref, cvb_ref, u_ref):
    mod = mod_ref[0]
    h = _rmsnorm(x_ref[...], g_ref[...]) * (1.0 + mod[1:2]) + mod[0:1]
    hb = h.astype(BF16)
    w = NA_WIDTH
    naqkv_ref[:, 0:w] = (_dot(hb, wa_ref[:, 0:w]) * NA_SCALE).astype(naqkv_ref.dtype)
    naqkv_ref[:, w:2 * w] = _dot(hb, wa_ref[:, w:2 * w]).astype(naqkv_ref.dtype)
    naqkv_ref[:, 2 * w:3 * w] = _dot(hb, wa_ref[:, 2 * w:3 * w]).astype(naqkv_ref.dtype)
    o = 3 * w
    m = M_WIDTH
    mqkv_ref[:, 0:m] = (_dot(hb, wa_ref[:, o:o + m]) * M_SCALE).astype(BF16)
    mqkv_ref[:, m:2 * m] = _dot(hb, wa_ref[:, o + m:o + 2 * m]).astype(BF16)
    mqkv_ref[:, 2 * m:3 * m] = _dot(hb, wa_ref[:, o + 2 * m:o + 3 * m]).astype(BF16)
    if_ref[...] = _dot(hb, wif_ref[...]) + bif_ref[...]
    c = CONV_WIDTH
    cvb_ref[...] = _dot(hb, wcv_ref[:, 0:c])
    u_ref[...] = _dot(hb, wcv_ref[:, c:2 * c]) * _dot(hb, wcv_ref[:, 2 * c:3 * c])


def _inproj(x, mod, g_pre, wa, wif, bif, wcv, *, seq_len, na_dtype):
    ntok, d = x.shape
    tile = min(TOKEN_TILE, ntok)
    per_seq = mod.shape[0] > 1
    assert ntok % tile == 0 and (not per_seq or seq_len % tile == 0)
    row = lambda n: pl.BlockSpec((tile, n), lambda i: (i, 0))
    return pl.pallas_call(
        _inproj_kernel,
        grid=(ntok // tile,),
        in_specs=[
            row(d),
            pl.BlockSpec((1, N_MOD, d), _mod_index_map(per_seq, max(seq_len // tile, 1))),
            _const_spec(g_pre.shape),
            _const_spec(wa.shape),
            _const_spec(wif.shape),
            _const_spec(bif.shape),
            _const_spec(wcv.shape),
        ],
        out_specs=[row(3 * NA_WIDTH), row(3 * M_WIDTH), row(LANES), row(CONV_WIDTH), row(CONV_WIDTH)],
        out_shape=[
            jax.ShapeDtypeStruct((ntok, 3 * NA_WIDTH), na_dtype),
            jax.ShapeDtypeStruct((ntok, 3 * M_WIDTH), BF16),
            jax.ShapeDtypeStruct((ntok, LANES), F32),
            jax.ShapeDtypeStruct((ntok, CONV_WIDTH), F32),
            jax.ShapeDtypeStruct((ntok, CONV_WIDTH), F32),
        ],
        compiler_params=_params("parallel"),
        name="in_projection",
    )(x, mod, g_pre, wa, wif, bif, wcv)


def _softmax_pv(scores, values):
    mx = functools.reduce(jnp.maximum, [jnp.max(s, axis=-1, keepdims=True) for s in scores])
    es = [jnp.exp(s - mx) for s in scores]
    den = functools.reduce(jnp.add, [jnp.sum(e, axis=-1, keepdims=True) for e in es])
    acc = functools.reduce(jnp.add, [_dot(e.astype(BF16), v) for e, v in zip(es, values)])
    return acc / den


def _ctx_attn_kernel(q_ref, k_ref, v_ref, o_ref):
    low = lax.broadcasted_iota(jnp.int32, (1, LANES), 1) < NA_HEAD_DIM
    for p in range(NA_WIDTH // LANES):
        sl = slice(p * LANES, (p + 1) * LANES)
        q2 = q_ref[0, :, sl].astype(BF16)
        k2 = k_ref[0, :, sl].astype(BF16)
        v2 = v_ref[0, :, sl].astype(BF16)
        halves = []
        for first in (True, False):
            qm = jnp.where(low == first, q2, jnp.zeros_like(q2))
            halves.append(_softmax_pv([_dot_nt(qm, k2)], [v2]))
        o_ref[0, :, sl] = jnp.where(low, halves[0], halves[1]).astype(o_ref.dtype)


def _context_attention(naqkv, batch, seq):
    a = naqkv.reshape(batch, seq, 3 * NA_WIDTH)
    spec = lambda j: pl.BlockSpec((1, seq, NA_WIDTH), lambda b: (b, 0, j))
    out = pl.pallas_call(
        _ctx_attn_kernel,
        grid=(batch,),
        in_specs=[spec(0), spec(1), spec(2)],
        out_specs=spec(0),
        out_shape=jax.ShapeDtypeStruct((batch, seq, NA_WIDTH), BF16),
        compiler_params=_params("parallel"),
        name="context_attention",
    )(a, a, a)
    return out.reshape(batch * seq, NA_WIDTH)


def _na_geometry(rows):
    r = NA_ROWS_PER_BLOCK
    slab = r + NA_KH - 1
    assert rows % r == 0 and rows >= slab and rows >= NA_KH
    nblk = rows // r
    pats = []
    for i in range(nblk):
        r0 = r * i
        s0 = int(np.clip(r0 - NA_KH // 2, 0, rows - slab))
        q_rows = r0 + np.arange(r)
        k_rows = s0 + np.arange(slab)
        rs = np.clip(q_rows - NA_KH // 2, 0, rows - NA_KH)
        valid = (k_rows[None, :] >= rs[:, None]) & (k_rows[None, :] < rs[:, None] + NA_KH)
        dr = np.clip(k_rows[None, :] - q_rows[:, None] + NA_KH - 1, 0, 2 * NA_KH - 2)
        pats.append((valid, dr))
    same = lambda a, b: np.array_equal(a[0], b[0]) and np.array_equal(a[1][a[0]], b[1][b[0]])
    assert nblk >= 3 and all(same(pats[1], p) for p in pats[1:-1])
    return r, slab, nblk, [pats[0], pats[1], pats[-1]]


def _na_bias_tables(rpb, rows):
    r, slab, _, pats = _na_geometry(rows)
    c = np.arange(GRID_W)
    cs = np.clip(c - NA_KW // 2, 0, GRID_W - NA_KW)
    kc = np.arange(GRID_W)
    col_ok = (kc[None, :] >= cs[:, None]) & (kc[None, :] < cs[:, None] + NA_KW)
    dc = np.clip(kc[None, :] - c[:, None] + NA_KW - 1, 0, 2 * NA_KW - 2)
    t4 = jnp.where(col_ok[None, None], rpb.astype(F32)[:, :, dc], NEG)
    out = []
    for valid, dr in pats:
        b = t4[:, dr]
        b = jnp.where(valid[None, :, :, None, None], b, NEG)
        out.append(b.transpose(0, 1, 3, 2, 4).reshape(rpb.shape[0], r * GRID_W, slab * GRID_W))
    return jnp.stack(out)


def _na_kernel(q_ref, k_ref, v_ref, kc_ref, vc_ref, bias_ref, o_ref, *, rows):
    r, slab, _, _ = _na_geometry(rows)
    i = pl.program_id(1)
    s0 = jnp.clip(r * i - NA_KH // 2, 0, rows - slab)
    start = pl.multiple_of(s0 * GRID_W, GRID_W)
    n_slab = slab * GRID_W
    low = lax.broadcasted_iota(jnp.int32, (1, LANES), 1) < NA_HEAD_DIM
    for p in range(NA_WIDTH // LANES):
        sl = slice(p * LANES, (p + 1) * LANES)
        q2 = q_ref[0, :, sl]
        k2 = k_ref[0, pl.ds(start, n_slab), sl]
        v2 = v_ref[0, pl.ds(start, n_slab), sl]
        kc2 = kc_ref[0, :, sl]
        vc2 = vc_ref[0, :, sl]
        halves = []
        for half, first in enumerate((True, False)):
            qm = jnp.where(low == first, q2, jnp.zeros_like(q2))
            s_loc = _dot_nt(qm, k2) + bias_ref[0, 2 * p + half]
            s_ctx = _dot_nt(qm, kc2)
            halves.append(_softmax_pv([s_loc, s_ctx], [v2, vc2]))
        o_ref[0, :, sl] = jnp.where(low, halves[0], halves[1]).astype(o_ref.dtype)


def _neighborhood_attention(naqkv, k_ctx, v_ctx, bias, batch, n):
    rows = n // GRID_W
    r, slab, nblk, _ = _na_geometry(rows)
    a = naqkv.reshape(batch, n, 3 * NA_WIDTH)
    past = k_ctx.shape[1]
    qb = r * GRID_W
    whole = lambda j: pl.BlockSpec((1, n, NA_WIDTH), lambda b, i: (b, 0, j))
    ctx = pl.BlockSpec((1, past, NA_WIDTH), lambda b, i: (b, 0, 0))
    pattern = lambda b, i: (jnp.where(i == 0, 0, jnp.where(i == nblk - 1, 2, 1)), 0, 0, 0)
    out = pl.pallas_call(
        functools.partial(_na_kernel, rows=rows),
        grid=(batch, nblk),
        in_specs=[
            pl.BlockSpec((1, qb, NA_WIDTH), lambda b, i: (b, i, 0)),
            whole(1), whole(2), ctx, ctx,
            pl.BlockSpec((1, NA_HEADS, qb, slab * GRID_W), pattern),
        ],
        out_specs=pl.BlockSpec((1, qb, NA_WIDTH), lambda b, i: (b, i, 0)),
        out_shape=jax.ShapeDtypeStruct((batch, n, NA_WIDTH), BF16),
        compiler_params=_params("parallel", "arbitrary"),
        name="neighborhood_attention",
    )(a, a, a, k_ctx, v_ctx, bias)
    return out.reshape(batch * n, NA_WIDTH)


def _mlstm_kernel(qf_ref, kf_ref, vf_ref, qb_ref, kb_ref, vb_ref, icf_ref, icb_ref, irf_ref, irb_ref,
                  c0_ref, n0_ref, m0_ref,
                  hf_ref, hb_ref, c_out_ref, n_out_ref, m_out_ref,
                  c_scr, n_scr, m_scr):
    step = pl.program_id(1)
    length = M_CHUNK
    nh = M_HEADS

    @pl.when(step == 0)
    def _():
        c_scr[...] = c0_ref[0]
        n_scr[...] = n0_ref[0]
        m_scr[...] = m0_ref[0]

    t_idx = lax.broadcasted_iota(jnp.int32, (length, length), 0)
    s_idx = lax.broadcasted_iota(jnp.int32, (length, length), 1)
    lower = (s_idx <= t_idx).astype(F32)
    upper = (s_idx >= t_idx).astype(F32)
    hi = lax.Precision.HIGHEST

    for direction in range(2):
        q_ref, k_ref, v_ref = (qf_ref, kf_ref, vf_ref) if direction == 0 else (qb_ref, kb_ref, vb_ref)
        ic_ref, ir_ref = (icf_ref, irf_ref) if direction == 0 else (icb_ref, irb_ref)
        h_ref = hf_ref if direction == 0 else hb_ref
        tri_col, tri_row = (lower, upper) if direction == 0 else (upper, lower)
        pre_col = ic_ref[0]
        pre_row = ir_ref[0]
        b_cols = jnp.dot(tri_col, _log_sigmoid(pre_col), precision=hi, preferred_element_type=F32)
        b_rows = jnp.dot(_log_sigmoid(pre_row), tri_row, precision=hi, preferred_element_type=F32)
        last = length - 1 if direction == 0 else 0
        for head in range(nh):
            j = direction * nh + head
            sl = slice(head * M_DK, (head + 1) * M_DK)
            q = q_ref[0, :, sl]
            k = k_ref[0, :, sl]
            v = v_ref[0, :, sl]
            i_col = pre_col[:, j:j + 1]
            i_row = pre_row[j:j + 1, :]
            b_col = b_cols[:, 2 * nh + j:2 * nh + j + 1]
            b_row = b_rows[2 * nh + j:2 * nh + j + 1, :]
            b_tot = b_row[:, last:last + 1]
            c_state = c_scr[j]
            n_state = n_scr[j:j + 1, :]
            m_state = m_scr[j:j + 1, 0:1]

            d = jnp.where(tri_col > 0.0, b_col - (b_row - i_row), NEG)
            inter = b_col + m_state
            mt = jnp.maximum(inter, jnp.max(d, axis=-1, keepdims=True))
            w_int = jnp.exp(inter - mt)
            s = _dot_nt(q, k) * jnp.exp(d - mt)
            num = w_int * _dot(q, c_state.astype(BF16)) + _dot(s.astype(BF16), v)
            den = (w_int * jnp.sum(q.astype(F32) * n_state, axis=-1, keepdims=True)
                   + jnp.sum(s, axis=-1, keepdims=True))
            h_ref[0, :, sl] = num / jnp.maximum(jnp.abs(den), jnp.exp(-mt))

            g = b_tot - b_col + i_col
            m_new = jnp.maximum(b_tot + m_state, jnp.max(g, axis=0, keepdims=True))
            decay = jnp.exp(b_tot + m_state - m_new)
            kw = k.astype(F32) * jnp.exp(g - m_new)
            c_scr[j] = decay * c_state + _dot_tn(kw.astype(BF16), v)
            n_scr[j:j + 1, :] = decay * n_state + jnp.sum(kw, axis=0, keepdims=True)
            m_scr[j:j + 1, :] = jnp.broadcast_to(m_new, (1, LANES))

    @pl.when(step == pl.num_programs(1) - 1)
    def _():
        c_out_ref[0] = c_scr[...]
        n_out_ref[0] = n_scr[...]
        m_out_ref[0] = m_scr[...]


def _bidir_mlstm(mqkv, ifpre, c0, n0, m0, batch, n):
    nc = n // M_CHUNK
    nh2 = 2 * M_HEADS
    a = mqkv.reshape(batch, n, 3 * M_WIDTH)
    ifc = ifpre.reshape(batch, n, LANES)
    ifr = ifc[:, :, :2 * nh2].transpose(0, 2, 1)
    fwd = lambda b, c: c
    bwd = lambda b, c: nc - 1 - c
    qkv = lambda j, at: pl.BlockSpec((1, M_CHUNK, M_WIDTH), lambda b, c: (b, at(b, c), j))
    col = lambda at: pl.BlockSpec((1, M_CHUNK, LANES), lambda b, c: (b, at(b, c), 0))
    rowl = lambda at: pl.BlockSpec((1, 2 * nh2, M_CHUNK), lambda b, c: (b, 0, at(b, c)))
    c_spec = pl.BlockSpec((1, nh2, M_DK, M_DV), lambda b, c: (b, 0, 0, 0))
    s_spec = pl.BlockSpec((1, nh2, LANES), lambda b, c: (b, 0, 0))
    h_spec = lambda at: pl.BlockSpec((1, M_CHUNK, M_WIDTH), lambda b, c: (b, at(b, c), 0))
    hf, hb, c_out, n_out, m_out = pl.pallas_call(
        _mlstm_kernel,
        grid=(batch, nc),
        in_specs=[qkv(0, fwd), qkv(1, fwd), qkv(2, fwd), qkv(0, bwd), qkv(1, bwd), qkv(2, bwd),
                  col(fwd), col(bwd), rowl(fwd), rowl(bwd), c_spec, s_spec, s_spec],
        out_specs=[h_spec(fwd), h_spec(bwd), c_spec, s_spec, s_spec],
        out_shape=[
            jax.ShapeDtypeStruct((batch, n, M_WIDTH), F32),
            jax.ShapeDtypeStruct((batch, n, M_WIDTH), F32),
            jax.ShapeDtypeStruct(c0.shape, F32),
            jax.ShapeDtypeStruct((batch, nh2, LANES), F32),
            jax.ShapeDtypeStruct((batch, nh2, LANES), F32),
        ],
        scratch_shapes=[
            pltpu.VMEM((nh2, M_DK, M_DV), F32),
            pltpu.VMEM((nh2, LANES), F32),
            pltpu.VMEM((nh2, LANES), F32),
        ],
        compiler_params=_params("parallel", "arbitrary"),
        name="bidir_mlstm",
    )(a, a, a, a, a, a, ifc, ifc, ifr, ifr, c0, n0, m0)
    return hf.reshape(batch * n, M_WIDTH), hb.reshape(batch * n, M_WIDTH), c_out, n_out, m_out


def _merge_kernel(x_ref, mod_ref, att_ref, hf_ref, hb_ref, cvb_ref, u_ref, uprev_ref, unext_ref,
                  gpre_ref, gpost_ref, mng_ref, convw_ref, bgate_ref,
                  wmo_ref, wgate_ref, wna_ref, wm_ref, wcv_ref, wout_ref,
                  o_ref, *, seq_len):
    tile, d = x_ref.shape
    x = x_ref[...]
    mod = mod_ref[0]
    h = _rmsnorm(x, gpre_ref[...]) * (1.0 + mod[1:2]) + mod[0:1]
    hb16 = h.astype(BF16)

    m_o = _dot(hb16, wmo_ref[...])
    hm_parts = []
    for head in range(M_HEADS):
        sl = slice(head * M_DV, (head + 1) * M_DV)
        hh = hf_ref[:, sl] + hb_ref[:, sl]
        hm_parts.append(hh * lax.rsqrt(jnp.mean(hh * hh, axis=-1, keepdims=True) + EPS))
    hm = jnp.concatenate(hm_parts, axis=-1) * mng_ref[...] * _sigmoid(m_o)

    u = u_ref[...]
    row = lax.broadcasted_iota(jnp.int32, (tile, 1), 0)
    pos = (pl.program_id(0) * tile + row) % seq_len
    u_before = jnp.where(row == 0, uprev_ref[SUBLANES - 1:SUBLANES, :], pltpu.roll(u, 1, 0))
    u_after = jnp.where(row == tile - 1, unext_ref[0:1, :], pltpu.roll(u, tile - 1, 0))
    u_before = jnp.where(pos == 0, 0.0, u_before)
    u_after = jnp.where(pos == seq_len - 1, 0.0, u_after)
    cw = convw_ref[...]
    hc = cvb_ref[...] * (u_before * cw[0:1] + u * cw[1:2] + u_after * cw[2:3])

    bg = bgate_ref[...]
    merged = (_sigmoid(_dot(hb16, wgate_ref[:, 0:d]) + bg[:, 0:d]) * _dot(att_ref[...], wna_ref[...])
              + _sigmoid(_dot(hb16, wgate_ref[:, d:2 * d]) + bg[:, d:2 * d]) * _dot(hm.astype(BF16), wm_ref[...])
              + _sigmoid(_dot(hb16, wgate_ref[:, 2 * d:3 * d]) + bg[:, 2 * d:3 * d])
              * _dot(hc.astype(BF16), wcv_ref[...]))
    y = _dot(merged.astype(BF16), wout_ref[...])
    o_ref[...] = x + mod[2:3] * _rmsnorm(y, gpost_ref[...])


def _merge(x, mod, att, hf, hb, cvb, u, consts, *, seq_len):
    ntok, d = x.shape
    tile = min(TOKEN_TILE, ntok)
    per_seq = mod.shape[0] > 1
    assert ntok % tile == 0 and (not per_seq or seq_len % tile == 0)
    assert tile % seq_len == 0 or seq_len % tile == 0
    row = lambda n: pl.BlockSpec((tile, n), lambda i: (i, 0))
    halo = tile // SUBLANES
    last_halo = ntok // SUBLANES - 1
    return pl.pallas_call(
        functools.partial(_merge_kernel, seq_len=seq_len),
        grid=(ntok // tile,),
        in_specs=[
            row(d),
            pl.BlockSpec((1, N_MOD, d), _mod_index_map(per_seq, max(seq_len // tile, 1))),
            row(NA_WIDTH), row(M_WIDTH), row(M_WIDTH), row(CONV_WIDTH), row(CONV_WIDTH),
            pl.BlockSpec((SUBLANES, CONV_WIDTH), lambda i: (jnp.maximum(i * halo - 1, 0), 0)),
            pl.BlockSpec((SUBLANES, CONV_WIDTH), lambda i: (jnp.minimum((i + 1) * halo, last_halo), 0)),
        ] + [_const_spec(c.shape) for c in consts],
        out_specs=row(d),
        out_shape=jax.ShapeDtypeStruct((ntok, d), F32),
        compiler_params=_params("parallel"),
        name="branch_merge",
    )(x, mod, att, hf, hb, cvb, u, u, u, *consts)


def _ffn_kernel(x_ref, mod_ref, gpre_ref, gpost_ref, wg_ref, wu_ref, wd_ref, o_ref):
    x = x_ref[...]
    mod = mod_ref[0]
    h = _rmsnorm(x, gpre_ref[...]) * (1.0 + mod[4:5]) + mod[3:4]
    hb16 = h.astype(BF16)
    d_ff = wg_ref.shape[1]
    chunk = d_ff // FFN_CHUNKS
    ff = None
    for c in range(FFN_CHUNKS):
        sl = slice(c * chunk, (c + 1) * chunk)
        gate = _dot(hb16, wg_ref[:, sl])
        act = (gate * _sigmoid(gate) * _dot(hb16, wu_ref[:, sl])).astype(BF16)
        part = _dot(act, wd_ref[sl, :])
        ff = part if ff is None else ff + part
    o_ref[...] = x + mod[5:6] * _rmsnorm(ff, gpost_ref[...])


def _ffn(x, mod, g_pre, g_post, wg, wu, wd, *, seq_len):
    ntok, d = x.shape
    tile = min(TOKEN_TILE, ntok)
    per_seq = mod.shape[0] > 1
    assert ntok % tile == 0 and (not per_seq or seq_len % tile == 0)
    assert wg.shape[1] % (FFN_CHUNKS * LANES) == 0
    row = pl.BlockSpec((tile, d), lambda i: (i, 0))
    return pl.pallas_call(
        _ffn_kernel,
        grid=(ntok // tile,),
        in_specs=[row, pl.BlockSpec((1, N_MOD, d), _mod_index_map(per_seq, max(seq_len // tile, 1))),
                  _const_spec(g_pre.shape), _const_spec(g_post.shape),
                  _const_spec(wg.shape), _const_spec(wu.shape), _const_spec(wd.shape)],
        out_specs=row,
        out_shape=jax.ShapeDtypeStruct((ntok, d), F32),
        compiler_params=_params("parallel"),
        name="swiglu_ffn",
    )(x, mod, g_pre, g_post, wg, wu, wd)


def _layer_weights(l, w_in, b_gate, m_b_i, m_b_f, g_pre_mix, g_post_mix, g_pre_ffn, g_post_ffn, m_norm_g, conv_w,
                   w_br_na, w_br_m, w_br_cv, w_out, w_ffn_gate, w_ffn_up, w_ffn_down):
    w = w_in[l]
    d = w.shape[0]
    o_mo = 3 * NA_WIDTH + 3 * M_WIDTH
    o_if = o_mo + M_WIDTH
    o_cv = o_if + 4 * M_HEADS
    o_gate = o_cv + 3 * CONV_WIDTH
    assert w.shape[1] == o_gate + N_BRANCH * d
    wif = jnp.zeros((d, LANES), F32).at[:, :4 * M_HEADS].set(w[:, o_if:o_cv])
    bif = jnp.zeros((1, LANES), F32).at[0, :4 * M_HEADS].set(
        jnp.concatenate([m_b_i[l].reshape(-1), m_b_f[l].reshape(-1)]))
    vec = lambda a: a[l].reshape(1, -1)
    return dict(
        wa=w[:, :o_mo].astype(BF16), wif=wif.astype(BF16), bif=bif, wcv=w[:, o_cv:o_gate].astype(BF16),
        g_pre_mix=vec(g_pre_mix), g_pre_ffn=vec(g_pre_ffn), g_post_ffn=vec(g_post_ffn),
        merge_consts=(vec(g_pre_mix), vec(g_post_mix), vec(m_norm_g), conv_w[l], vec(b_gate),
                      w[:, o_mo:o_if].astype(BF16), w[:, o_gate:].astype(BF16),
                      w_br_na[l].astype(BF16), w_br_m[l].astype(BF16), w_br_cv[l].astype(BF16),
                      w_out[l].astype(BF16)),
        wg=w_ffn_gate[l].astype(BF16), wu=w_ffn_up[l].astype(BF16), wd=w_ffn_down[l].astype(BF16),
    )


def _trunk_layer(x, mod, lw, batch, seq_len, ctx):
    is_ctx = ctx is None
    naqkv, mqkv, ifpre, cvb, u = _inproj(x, mod, lw["g_pre_mix"], lw["wa"], lw["wif"], lw["bif"], lw["wcv"],
                                         seq_len=seq_len, na_dtype=F32 if is_ctx else BF16)
    if is_ctx:
        att = _context_attention(naqkv, batch, seq_len)
        c0 = jnp.zeros((batch, 2 * M_HEADS, M_DK, M_DV), F32)
        n0 = jnp.zeros((batch, 2 * M_HEADS, M_DK), F32)
        m0 = jnp.zeros((batch, 2 * M_HEADS, LANES), F32)
    else:
        k_ctx, v_ctx, bias, c0, n0, m0 = ctx
        att = _neighborhood_attention(naqkv, k_ctx, v_ctx, bias, batch, seq_len)
    hf, hb, c_new, n_new, m_new = _bidir_mlstm(mqkv, ifpre, c0, n0, m0, batch, seq_len)
    x = _merge(x, mod, att, hf, hb, cvb, u, lw["merge_consts"], seq_len=seq_len)
    x = _ffn(x, mod, lw["g_pre_ffn"], lw["g_post_ffn"], lw["wg"], lw["wu"], lw["wd"], seq_len=seq_len)
    return x, (naqkv, c_new, n_new, m_new)


def kernel(x_prompt, x_sample, c, cache_k, cache_v, state_C, state_n, state_m, c_ctx, w_ada, b_ada, g_pre_mix,
           g_post_mix, g_pre_ffn, g_post_ffn, w_in, b_gate, m_b_i, m_b_f, na_rpb, m_norm_g, conv_w, w_br_na,
           w_br_m, w_br_cv, w_out, w_ffn_gate, w_ffn_up, w_ffn_down):
    batch, seq, d = x_prompt.shape
    dec_batch, dec_seq, _ = x_sample.shape
    depth = w_in.shape[0]
    past = cache_k.shape[3]
    assert dec_batch + 1 <= COND_ROWS

    cond = jnp.zeros((COND_ROWS, d), F32).at[:dec_batch].set(c).at[dec_batch].set(c_ctx)
    mod = _ada_modulation(cond, w_ada, b_ada).reshape(depth, COND_ROWS, N_MOD, d)

    xp = x_prompt.reshape(batch * seq, d)
    xs = x_sample.reshape(dec_batch * dec_seq, d)
    ks, vs, cs, ns, ms = [], [], [], [], []
    for l in range(depth):
        lw = _layer_weights(l, w_in, b_gate, m_b_i, m_b_f, g_pre_mix, g_post_mix, g_pre_ffn, g_post_ffn,
                            m_norm_g, conv_w, w_br_na, w_br_m, w_br_cv, w_out, w_ffn_gate, w_ffn_up, w_ffn_down)
        xp, (naqkv, c_l, n_l, m_l) = _trunk_layer(xp, mod[l, dec_batch:dec_batch + 1], lw, batch, seq, None)
        heads = lambda a: a.reshape(batch, seq, NA_HEADS, NA_HEAD_DIM).transpose(0, 2, 1, 3)
        ks.append(heads(naqkv[:, NA_WIDTH:2 * NA_WIDTH]))
        vs.append(heads(naqkv[:, 2 * NA_WIDTH:]))
        cs.append(c_l.reshape(batch, 2, M_HEADS, M_DK, M_DV))
        ns.append(n_l.reshape(batch, 2, M_HEADS, M_DK))
        ms.append(m_l[:, :, 0].reshape(batch, 2, M_HEADS))

        tokens = lambda a: a.transpose(0, 2, 1, 3).reshape(dec_batch, past, NA_WIDTH).astype(BF16)
        ctx = (tokens(cache_k[:, l]), tokens(cache_v[:, l]), _na_bias_tables(na_rpb[l], dec_seq // GRID_W),
               state_C[:, l].reshape(dec_batch, 2 * M_HEADS, M_DK, M_DV), state_n[:, l].reshape(dec_batch, 2 * M_HEADS, M_DK),
               jnp.broadcast_to(state_m[:, l].reshape(dec_batch, 2 * M_HEADS, 1), (dec_batch, 2 * M_HEADS, LANES)))
        xs, _ = _trunk_layer(xs, mod[l, :dec_batch], lw, dec_batch, dec_seq, ctx)

    return (xp.reshape(batch, seq, d), xs.reshape(dec_batch, dec_seq, d),
            jnp.stack(ks, axis=1), jnp.stack(vs, axis=1), jnp.stack(cs, axis=1),
            jnp.stack(ns, axis=1), jnp.stack(ms, axis=1))
```

```python
import functools

import numpy as np
import jax
import jax.numpy as jnp
from jax import lax
from jax.experimental import pallas as pl
from jax.experimental.pallas import tpu as pltpu

F32 = jnp.float32
BF16 = jnp.bfloat16

GRID_W = 64
NA_HEADS = 8
NA_HEAD_DIM = 64
NA_WIDTH = NA_HEADS * NA_HEAD_DIM
NA_KH = 8
NA_KW = 16
NA_SCALE = NA_HEAD_DIM ** -0.5
M_HEADS = 4
M_DK = 128
M_DV = 128
M_WIDTH = M_HEADS * M_DV
M_CHUNK = 128
M_SCALE = M_DK ** -0.5
CONV_WIDTH = 512
N_BRANCH = 3
N_MOD = 6
EPS = 1e-6
NEG = -1e30

LANES = 128
SUBLANES = 8
VMEM_LIMIT_BYTES = 56 * 1024 * 1024

TOKEN_TILE = 512
NA_ROWS_PER_BLOCK = 4
FFN_CHUNKS = 2
ADA_TILE = 1024
COND_ROWS = 16


def _const_spec(shape):
    nd = len(shape)
    return pl.BlockSpec(shape, lambda *_: (0,) * nd, pipeline_mode=pl.Buffered(1))


def _params(*sem):
    return pltpu.CompilerParams(dimension_semantics=sem, vmem_limit_bytes=VMEM_LIMIT_BYTES)


def _sigmoid(x):
    return 1.0 / (1.0 + jnp.exp(-x))


def _log_sigmoid(x):
    return jnp.minimum(x, 0.0) - jnp.log1p(jnp.exp(-jnp.abs(x)))


def _rmsnorm(x, g):
    return x * lax.rsqrt(jnp.mean(x * x, axis=-1, keepdims=True) + EPS) * g


def _dot(a, b):
    return jnp.dot(a, b, preferred_element_type=F32)


def _dot_nt(a, b):
    return lax.dot_general(a, b, (((1,), (1,)), ((), ())), preferred_element_type=F32)


def _dot_tn(a, b):
    return lax.dot_general(a, b, (((0,), (0,)), ((), ())), preferred_element_type=F32)


def _ada_kernel(cond_ref, w_ref, b_ref, o_ref):
    c = cond_ref[...]
    s = (c * _sigmoid(c)).astype(BF16)
    o_ref[0] = _dot(s, w_ref[0].astype(BF16)) + b_ref[0]


def _ada_modulation(cond, w_ada, b_ada):
    depth, d, n = w_ada.shape
    return pl.pallas_call(
        _ada_kernel,
        grid=(depth, n // ADA_TILE),
        in_specs=[
            pl.BlockSpec((COND_ROWS, d), lambda l, j: (0, 0)),
            pl.BlockSpec((1, d, ADA_TILE), lambda l, j: (l, 0, j)),
            pl.BlockSpec((1, 1, ADA_TILE), lambda l, j: (l, 0, j)),
        ],
        out_specs=pl.BlockSpec((1, COND_ROWS, ADA_TILE), lambda l, j: (l, 0, j)),
        out_shape=jax.ShapeDtypeStruct((depth, COND_ROWS, n), F32),
        compiler_params=_params("arbitrary", "arbitrary"),
        name="ada_modulation",
    )(cond, w_ada, b_ada.reshape(depth, 1, n))


def _mod_index_map(per_seq, tiles_per_seq):
    if per_seq:
        return lambda i: (i // tiles_per_seq, 0, 0)
    return lambda i: (0, 0, 0)


def _inproj_kernel(x_ref, mod_ref, g_ref, wa_ref, wkt_ref, wift_ref, bift_ref, wcv_ref,
                   naqkv_ref, mqv_ref, mkt_ref, ift_ref, cvb_ref, u_ref):
    mod = mod_ref[0]
    h = _rmsnorm(x_ref[...], g_ref[...]) * (1.0 + mod[1:2]) + mod[0:1]
    hb = h.astype(BF16)
    w = NA_WIDTH
    naqkv_ref[:, 0:w] = (_dot(hb, wa_ref[:, 0:w]) * NA_SCALE).astype(naqkv_ref.dtype)
    naqkv_ref[:, w:2 * w] = _dot(hb, wa_ref[:, w:2 * w]).astype(naqkv_ref.dtype)
    naqkv_ref[:, 2 * w:3 * w] = _dot(hb, wa_ref[:, 2 * w:3 * w]).astype(naqkv_ref.dtype)
    o = 3 * w
    m = M_WIDTH
    mqv_ref[:, 0:m] = (_dot(hb, wa_ref[:, o:o + m]) * M_SCALE).astype(BF16)
    mqv_ref[:, m:2 * m] = _dot(hb, wa_ref[:, o + m:o + 2 * m]).astype(BF16)
    mkt_ref[...] = _dot_nt(wkt_ref[...], hb).astype(BF16)
    ift_ref[...] = _dot_nt(wift_ref[...], hb) + bift_ref[...]
    c = CONV_WIDTH
    cvb_ref[...] = _dot(hb, wcv_ref[:, 0:c])
    u_ref[...] = _dot(hb, wcv_ref[:, c:2 * c]) * _dot(hb, wcv_ref[:, 2 * c:3 * c])


def _inproj(x, mod, g_pre, wa, wkt, wift, bift, wcv, *, seq_len, na_dtype):
    ntok, d = x.shape
    tile = min(TOKEN_TILE, ntok)
    per_seq = mod.shape[0] > 1
    assert ntok % tile == 0 and (not per_seq or seq_len % tile == 0)
    row = lambda n: pl.BlockSpec((tile, n), lambda i: (i, 0))
    col = lambda n: pl.BlockSpec((n, tile), lambda i: (0, i))
    n_gate = 4 * M_HEADS
    return pl.pallas_call(
        _inproj_kernel,
        grid=(ntok // tile,),
        in_specs=[
            row(d),
            pl.BlockSpec((1, N_MOD, d), _mod_index_map(per_seq, max(seq_len // tile, 1))),
            _const_spec(g_pre.shape),
            _const_spec(wa.shape),
            _const_spec(wkt.shape),
            _const_spec(wift.shape),
            _const_spec(bift.shape),
            _const_spec(wcv.shape),
        ],
        out_specs=[row(3 * NA_WIDTH), row(2 * M_WIDTH), col(M_WIDTH), col(n_gate),
                   row(CONV_WIDTH), row(CONV_WIDTH)],
        out_shape=[
            jax.ShapeDtypeStruct((ntok, 3 * NA_WIDTH), na_dtype),
            jax.ShapeDtypeStruct((ntok, 2 * M_WIDTH), BF16),
            jax.ShapeDtypeStruct((M_WIDTH, ntok), BF16),
            jax.ShapeDtypeStruct((n_gate, ntok), F32),
            jax.ShapeDtypeStruct((ntok, CONV_WIDTH), F32),
            jax.ShapeDtypeStruct((ntok, CONV_WIDTH), F32),
        ],
        compiler_params=_params("parallel"),
        name="in_projection",
    )(x, mod, g_pre, wa, wkt, wift, bift, wcv)


def _softmax_pv(scores, values):
    mx = functools.reduce(jnp.maximum, [jnp.max(s, axis=-1, keepdims=True) for s in scores])
    es = [jnp.exp(s - mx) for s in scores]
    den = functools.reduce(jnp.add, [jnp.sum(e, axis=-1, keepdims=True) for e in es])
    acc = functools.reduce(jnp.add, [_dot(e.astype(BF16), v) for e, v in zip(es, values)])
    return acc / den


def _ctx_attn_kernel(q_ref, k_ref, v_ref, o_ref):
    low = lax.broadcasted_iota(jnp.int32, (1, LANES), 1) < NA_HEAD_DIM
    for p in range(NA_WIDTH // LANES):
        sl = slice(p * LANES, (p + 1) * LANES)
        q2 = q_ref[0, :, sl].astype(BF16)
        k2 = k_ref[0, :, sl].astype(BF16)
        v2 = v_ref[0, :, sl].astype(BF16)
        halves = []
        for first in (True, False):
            qm = jnp.where(low == first, q2, jnp.zeros_like(q2))
            halves.append(_softmax_pv([_dot_nt(qm, k2)], [v2]))
        o_ref[0, :, sl] = jnp.where(low, halves[0], halves[1]).astype(o_ref.dtype)


def _context_attention(naqkv, batch, seq):
    a = naqkv.reshape(batch, seq, 3 * NA_WIDTH)
    spec = lambda j: pl.BlockSpec((1, seq, NA_WIDTH), lambda b: (b, 0, j))
    out = pl.pallas_call(
        _ctx_attn_kernel,
        grid=(batch,),
        in_specs=[spec(0), spec(1), spec(2)],
        out_specs=spec(0),
        out_shape=jax.ShapeDtypeStruct((batch, seq, NA_WIDTH), BF16),
        compiler_params=_params("parallel"),
        name="context_attention",
    )(a, a, a)
    return out.reshape(batch * seq, NA_WIDTH)


def _na_geometry(rows):
    r = NA_ROWS_PER_BLOCK
    slab = r + NA_KH - 1
    assert rows % r == 0 and rows >= slab and rows >= NA_KH
    nblk = rows // r
    pats = []
    for i in range(nblk):
        r0 = r * i
        s0 = int(np.clip(r0 - NA_KH // 2, 0, rows - slab))
        q_rows = r0 + np.arange(r)
        k_rows = s0 + np.arange(slab)
        rs = np.clip(q_rows - NA_KH // 2, 0, rows - NA_KH)
        valid = (k_rows[None, :] >= rs[:, None]) & (k_rows[None, :] < rs[:, None] + NA_KH)
        dr = np.clip(k_rows[None, :] - q_rows[:, None] + NA_KH - 1, 0, 2 * NA_KH - 2)
        pats.append((valid, dr))
    same = lambda a, b: np.array_equal(a[0], b[0]) and np.array_equal(a[1][a[0]], b[1][b[0]])
    assert nblk >= 3 and all(same(pats[1], p) for p in pats[1:-1])
    return r, slab, nblk, [pats[0], pats[1], pats[-1]]


def _na_bias_tables(rpb, rows):
    r, slab, _, pats = _na_geometry(rows)
    c = np.arange(GRID_W)
    cs = np.clip(c - NA_KW // 2, 0, GRID_W - NA_KW)
    kc = np.arange(GRID_W)
    col_ok = (kc[None, :] >= cs[:, None]) & (kc[None, :] < cs[:, None] + NA_KW)
    dc = np.clip(kc[None, :] - c[:, None] + NA_KW - 1, 0, 2 * NA_KW - 2)
    t4 = jnp.where(col_ok[None, None], rpb.astype(F32)[:, :, dc], NEG)
    out = []
    for valid, dr in pats:
        b = t4[:, dr]
        b = jnp.where(valid[None, :, :, None, None], b, NEG)
        out.append(b.transpose(0, 1, 3, 2, 4).reshape(rpb.shape[0], r * GRID_W, slab * GRID_W))
    return jnp.stack(out)


def _na_kernel(q_ref, k_ref, v_ref, kc_ref, vc_ref, bias_ref, o_ref, *, rows):
    r, slab, _, _ = _na_geometry(rows)
    i = pl.program_id(1)
    s0 = jnp.clip(r * i - NA_KH // 2, 0, rows - slab)
    start = pl.multiple_of(s0 * GRID_W, GRID_W)
    n_slab = slab * GRID_W
    low = lax.broadcasted_iota(jnp.int32, (1, LANES), 1) < NA_HEAD_DIM
    for p in range(NA_WIDTH // LANES):
        sl = slice(p * LANES, (p + 1) * LANES)
        q2 = q_ref[0, :, sl]
        k2 = k_ref[0, pl.ds(start, n_slab), sl]
        v2 = v_ref[0, pl.ds(start, n_slab), sl]
        kc2 = kc_ref[0, :, sl]
        vc2 = vc_ref[0, :, sl]
        halves = []
        for half, first in enumerate((True, False)):
            qm = jnp.where(low == first, q2, jnp.zeros_like(q2))
            s_loc = _dot_nt(qm, k2) + bias_ref[0, 2 * p + half]
            s_ctx = _dot_nt(qm, kc2)
            halves.append(_softmax_pv([s_loc, s_ctx], [v2, vc2]))
        o_ref[0, :, sl] = jnp.where(low, halves[0], halves[1]).astype(o_ref.dtype)


def _neighborhood_attention(naqkv, k_ctx, v_ctx, bias, batch, n):
    rows = n // GRID_W
    r, slab, nblk, _ = _na_geometry(rows)
    a = naqkv.reshape(batch, n, 3 * NA_WIDTH)
    past = k_ctx.shape[1]
    qb = r * GRID_W
    whole = lambda j: pl.BlockSpec((1, n, NA_WIDTH), lambda b, i: (b, 0, j))
    ctx = pl.BlockSpec((1, past, NA_WIDTH), lambda b, i: (b, 0, 0))
    pattern = lambda b, i: (jnp.where(i == 0, 0, jnp.where(i == nblk - 1, 2, 1)), 0, 0, 0)
    out = pl.pallas_call(
        functools.partial(_na_kernel, rows=rows),
        grid=(batch, nblk),
        in_specs=[
            pl.BlockSpec((1, qb, NA_WIDTH), lambda b, i: (b, i, 0)),
            whole(1), whole(2), ctx, ctx,
            pl.BlockSpec((1, NA_HEADS, qb, slab * GRID_W), pattern),
        ],
        out_specs=pl.BlockSpec((1, qb, NA_WIDTH), lambda b, i: (b, i, 0)),
        out_shape=jax.ShapeDtypeStruct((batch, n, NA_WIDTH), BF16),
        compiler_params=_params("parallel", "arbitrary"),
        name="neighborhood_attention",
    )(a, a, a, k_ctx, v_ctx, bias)
    return out.reshape(batch * n, NA_WIDTH)


def _mlstm_kernel(qf_ref, vf_ref, ktf_ref, iff_ref, qb_ref, vb_ref, ktb_ref, ifb_ref, c0_ref, m0_ref,
                  hf_ref, hb_ref, c_out_ref, m_out_ref, c_scr, m_scr):
    step = pl.program_id(1)
    length = M_CHUNK
    nh = M_HEADS

    @pl.when(step == 0)
    def _():
        c_scr[...] = c0_ref[0]
        m_scr[...] = m0_ref[0]

    t_idx = lax.broadcasted_iota(jnp.int32, (length, length), 0)
    s_idx = lax.broadcasted_iota(jnp.int32, (length, length), 1)
    ones = jnp.ones((length, M_DV), BF16)

    chains = []
    for direction in range(2):
        q_ref, v_ref, kt_ref, if_ref = ((qf_ref, vf_ref, ktf_ref, iff_ref) if direction == 0
                                        else (qb_ref, vb_ref, ktb_ref, ifb_ref))
        earlier = (s_idx <= t_idx) if direction == 0 else (s_idx >= t_idx)
        pre = if_ref[...]
        log_f = _log_sigmoid(pre[2 * nh:4 * nh, :])
        tri = jnp.where((t_idx <= s_idx) if direction == 0 else (t_idx >= s_idx), 1.0, 0.0)
        b_rows = jnp.dot(log_f, tri, precision=lax.Precision.HIGHEST, preferred_element_type=F32)
        last = length - 1 if direction == 0 else 0
        for head in range(nh):
            j = direction * nh + head
            sl = slice(head * M_DK, (head + 1) * M_DK)
            b_row = b_rows[j:j + 1, :]
            chains.append(dict(
                j=j, sl=sl, q_ref=q_ref, v_ref=v_ref, kt_ref=kt_ref, earlier=earlier,
                h_ref=hf_ref if direction == 0 else hb_ref,
                f_row=log_f[j:j + 1, :],
                a_row=pre[j:j + 1, :] - b_row,
                b_tot=b_row[:, last:last + 1],
                m_state=m_scr[j:j + 1, 0:1]))

    for ch in chains:
        a_mask = jnp.where(ch["earlier"], ch["a_row"], NEG)
        m_col = jnp.maximum(ch["m_state"], jnp.max(a_mask, axis=-1, keepdims=True))
        b_col = jnp.sum(jnp.where(ch["earlier"], ch["f_row"], 0.0), axis=-1, keepdims=True)
        ch["a_mask"] = a_mask
        ch["m_rep"] = jnp.broadcast_to(m_col, (length, M_DV))
        ch["mt_rep"] = jnp.broadcast_to(b_col + m_col, (length, M_DV))
    for ch in chains:
        q = ch["q_ref"][:, ch["sl"]]
        s = _dot(q, ch["kt_ref"][ch["sl"], :]) * jnp.exp(ch.pop("a_mask") - ch["m_rep"])
        ch["q"] = q
        ch["s"] = s.astype(BF16)
    for ch in chains:
        v_aug = jnp.concatenate([ch["v_ref"][:, ch["sl"]], ones], axis=-1)
        w_int = jnp.exp(ch["m_state"] - ch["m_rep"])
        inter = _dot(ch.pop("q"), c_scr[ch["j"]].astype(BF16))
        intra = _dot(ch.pop("s"), v_aug)
        num = w_int * inter[:, :M_DV] + intra[:, :M_DV]
        den = w_int * inter[:, M_DV:] + intra[:, M_DV:]
        ch["h_ref"][:, ch["sl"]] = num / jnp.maximum(jnp.abs(den), jnp.exp(-ch["mt_rep"]))
        ch["v_aug"] = v_aug
    for ch in chains:
        j = ch["j"]
        g_row = ch["b_tot"] + ch["a_row"]
        m_new = jnp.maximum(ch["b_tot"] + ch["m_state"], jnp.max(g_row, axis=-1, keepdims=True))
        decay = jnp.exp(ch["b_tot"] + ch["m_state"] - m_new)
        kw = (ch["kt_ref"][ch["sl"], :].astype(F32) * jnp.exp(g_row - m_new)).astype(BF16)
        c_scr[j] = decay * c_scr[j] + _dot(kw, ch["v_aug"])
        m_scr[j:j + 1, :] = jnp.broadcast_to(m_new, (1, LANES))

    @pl.when(step == pl.num_programs(1) - 1)
    def _():
        c_out_ref[0] = c_scr[...]
        m_out_ref[0] = m_scr[...]


def _bidir_mlstm(mqv, mkt, ift, c0, m0, batch, n):
    nc = n // M_CHUNK
    nh2 = 2 * M_HEADS
    fwd = lambda b, c: b * nc + c
    bwd = lambda b, c: b * nc + nc - 1 - c
    tok = lambda j, at: pl.BlockSpec((M_CHUNK, M_WIDTH), lambda b, c: (at(b, c), j))
    feat = lambda rows, at: pl.BlockSpec((rows, M_CHUNK), lambda b, c: (0, at(b, c)))
    c_spec = pl.BlockSpec((1, nh2, M_DK, 2 * M_DV), lambda b, c: (b, 0, 0, 0))
    m_spec = pl.BlockSpec((1, nh2, LANES), lambda b, c: (b, 0, 0))
    stream = lambda at: [tok(0, at), tok(1, at), feat(M_WIDTH, at), feat(2 * nh2, at)]
    return pl.pallas_call(
        _mlstm_kernel,
        grid=(batch, nc),
        in_specs=stream(fwd) + stream(bwd) + [c_spec, m_spec],
        out_specs=[tok(0, fwd), tok(0, bwd), c_spec, m_spec],
        out_shape=[
            jax.ShapeDtypeStruct((batch * n, M_WIDTH), F32),
            jax.ShapeDtypeStruct((batch * n, M_WIDTH), F32),
            jax.ShapeDtypeStruct(c0.shape, F32),
            jax.ShapeDtypeStruct(m0.shape, F32),
        ],
        scratch_shapes=[
            pltpu.VMEM((nh2, M_DK, 2 * M_DV), F32),
            pltpu.VMEM((nh2, LANES), F32),
        ],
        compiler_params=_params("parallel", "arbitrary"),
        name="bidir_mlstm",
    )(mqv, mqv, mkt, ift, mqv, mqv, mkt, ift, c0, m0)


def _merge_kernel(x_ref, mod_ref, att_ref, hf_ref, hb_ref, cvb_ref, u_ref, uprev_ref, unext_ref,
                  gpre_ref, gpost_ref, mng_ref, convw_ref, bgate_ref,
                  wmo_ref, wgate_ref, wna_ref, wm_ref, wcv_ref, wout_ref,
                  o_ref, *, seq_len):
    tile, d = x_ref.shape
    x = x_ref[...]
    mod = mod_ref[0]
    h = _rmsnorm(x, gpre_ref[...]) * (1.0 + mod[1:2]) + mod[0:1]
    hb16 = h.astype(BF16)

    m_o = _dot(hb16, wmo_ref[...])
    hm_parts = []
    for head in range(M_HEADS):
        sl = slice(head * M_DV, (head + 1) * M_DV)
        hh = hf_ref[:, sl] + hb_ref[:, sl]
        hm_parts.append(hh * lax.rsqrt(jnp.mean(hh * hh, axis=-1, keepdims=True) + EPS))
    hm = jnp.concatenate(hm_parts, axis=-1) * mng_ref[...] * _sigmoid(m_o)

    u = u_ref[...]
    row = lax.broadcasted_iota(jnp.int32, (tile, 1), 0)
    pos = (pl.program_id(0) * tile + row) % seq_len
    u_before = jnp.where(row == 0, uprev_ref[SUBLANES - 1:SUBLANES, :], pltpu.roll(u, 1, 0))
    u_after = jnp.where(row == tile - 1, unext_ref[0:1, :], pltpu.roll(u, tile - 1, 0))
    u_before = jnp.where(pos == 0, 0.0, u_before)
    u_after = jnp.where(pos == seq_len - 1, 0.0, u_after)
    cw = convw_ref[...]
    hc = cvb_ref[...] * (u_before * cw[0:1] + u * cw[1:2] + u_after * cw[2:3])

    bg = bgate_ref[...]
    merged = (_sigmoid(_dot(hb16, wgate_ref[:, 0:d]) + bg[:, 0:d]) * _dot(att_ref[...], wna_ref[...])
              + _sigmoid(_dot(hb16, wgate_ref[:, d:2 * d]) + bg[:, d:2 * d]) * _dot(hm.astype(BF16), wm_ref[...])
              + _sigmoid(_dot(hb16, wgate_ref[:, 2 * d:3 * d]) + bg[:, 2 * d:3 * d])
              * _dot(hc.astype(BF16), wcv_ref[...]))
    y = _dot(merged.astype(BF16), wout_ref[...])
    o_ref[...] = x + mod[2:3] * _rmsnorm(y, gpost_ref[...])


def _merge(x, mod, att, hf, hb, cvb, u, consts, *, seq_len):
    ntok, d = x.shape
    tile = min(TOKEN_TILE, ntok)
    per_seq = mod.shape[0] > 1
    assert ntok % tile == 0 and (not per_seq or seq_len % tile == 0)
    assert tile % seq_len == 0 or seq_len % tile == 0
    row = lambda n: pl.BlockSpec((tile, n), lambda i: (i, 0))
    halo = tile // SUBLANES
    last_halo = ntok // SUBLANES - 1
    return pl.pallas_call(
        functools.partial(_merge_kernel, seq_len=seq_len),
        grid=(ntok // tile,),
        in_specs=[
            row(d),
            pl.BlockSpec((1, N_MOD, d), _mod_index_map(per_seq, max(seq_len // tile, 1))),
            row(NA_WIDTH), row(M_WIDTH), row(M_WIDTH), row(CONV_WIDTH), row(CONV_WIDTH),
            pl.BlockSpec((SUBLANES, CONV_WIDTH), lambda i: (jnp.maximum(i * halo - 1, 0), 0)),
            pl.BlockSpec((SUBLANES, CONV_WIDTH), lambda i: (jnp.minimum((i + 1) * halo, last_halo), 0)),
        ] + [_const_spec(c.shape) for c in consts],
        out_specs=row(d),
        out_shape=jax.ShapeDtypeStruct((ntok, d), F32),
        compiler_params=_params("parallel"),
        name="branch_merge",
    )(x, mod, att, hf, hb, cvb, u, u, u, *consts)


def _ffn_kernel(x_ref, mod_ref, gpre_ref, gpost_ref, wg_ref, wu_ref, wd_ref, o_ref):
    x = x_ref[...]
    mod = mod_ref[0]
    h = _rmsnorm(x, gpre_ref[...]) * (1.0 + mod[4:5]) + mod[3:4]
    hb16 = h.astype(BF16)
    d_ff = wg_ref.shape[1]
    chunk = d_ff // FFN_CHUNKS
    ff = None
    for c in range(FFN_CHUNKS):
        sl = slice(c * chunk, (c + 1) * chunk)
        gate = _dot(hb16, wg_ref[:, sl])
        act = (gate * _sigmoid(gate) * _dot(hb16, wu_ref[:, sl])).astype(BF16)
        part = _dot(act, wd_ref[sl, :])
        ff = part if ff is None else ff + part
    o_ref[...] = x + mod[5:6] * _rmsnorm(ff, gpost_ref[...])


def _ffn(x, mod, g_pre, g_post, wg, wu, wd, *, seq_len):
    ntok, d = x.shape
    tile = min(TOKEN_TILE, ntok)
    per_seq = mod.shape[0] > 1
    assert ntok % tile == 0 and (not per_seq or seq_len % tile == 0)
    assert wg.shape[1] % (FFN_CHUNKS * LANES) == 0
    row = pl.BlockSpec((tile, d), lambda i: (i, 0))
    return pl.pallas_call(
        _ffn_kernel,
        grid=(ntok // tile,),
        in_specs=[row, pl.BlockSpec((1, N_MOD, d), _mod_index_map(per_seq, max(seq_len // tile, 1))),
                  _const_spec(g_pre.shape), _const_spec(g_post.shape),
                  _const_spec(wg.shape), _const_spec(wu.shape), _const_spec(wd.shape)],
        out_specs=row,
        out_shape=jax.ShapeDtypeStruct((ntok, d), F32),
        compiler_params=_params("parallel"),
        name="swiglu_ffn",
    )(x, mod, g_pre, g_post, wg, wu, wd)


def _layer_weights(l, w_in, b_gate, m_b_i, m_b_f, g_pre_mix, g_post_mix, g_pre_ffn, g_post_ffn, m_norm_g, conv_w,
                   w_br_na, w_br_m, w_br_cv, w_out, w_ffn_gate, w_ffn_up, w_ffn_down):
    w = w_in[l]
    d = w.shape[0]
    o_mo = 3 * NA_WIDTH + 3 * M_WIDTH
    o_if = o_mo + M_WIDTH
    o_cv = o_if + 4 * M_HEADS
    o_gate = o_cv + 3 * CONV_WIDTH
    assert w.shape[1] == o_gate + N_BRANCH * d
    o_mk = 3 * NA_WIDTH + M_WIDTH
    o_mv = o_mk + M_WIDTH
    bift = jnp.concatenate([m_b_i[l].reshape(-1), m_b_f[l].reshape(-1)]).reshape(4 * M_HEADS, 1)
    vec = lambda a: a[l].reshape(1, -1)
    return dict(
        wa=jnp.concatenate([w[:, :o_mk], w[:, o_mv:o_mo]], axis=1).astype(BF16),
        wkt=w[:, o_mk:o_mv].T.astype(BF16), wift=w[:, o_if:o_cv].T.astype(BF16), bift=bift,
        wcv=w[:, o_cv:o_gate].astype(BF16),
        g_pre_mix=vec(g_pre_mix), g_pre_ffn=vec(g_pre_ffn), g_post_ffn=vec(g_post_ffn),
        merge_consts=(vec(g_pre_mix), vec(g_post_mix), vec(m_norm_g), conv_w[l], vec(b_gate),
                      w[:, o_mo:o_if].astype(BF16), w[:, o_gate:].astype(BF16),
                      w_br_na[l].astype(BF16), w_br_m[l].astype(BF16), w_br_cv[l].astype(BF16),
                      w_out[l].astype(BF16)),
        wg=w_ffn_gate[l].astype(BF16), wu=w_ffn_up[l].astype(BF16), wd=w_ffn_down[l].astype(BF16),
    )


def _trunk_layer(x, mod, lw, batch, seq_len, ctx):
    is_ctx = ctx is None
    naqkv, mqv, mkt, ift, cvb, u = _inproj(x, mod, lw["g_pre_mix"], lw["wa"], lw["wkt"], lw["wift"], lw["bift"],
                                           lw["wcv"], seq_len=seq_len, na_dtype=F32 if is_ctx else BF16)
    if is_ctx:
        att = _context_attention(naqkv, batch, seq_len)
        c0 = jnp.zeros((batch, 2 * M_HEADS, M_DK, 2 * M_DV), F32)
        m0 = jnp.zeros((batch, 2 * M_HEADS, LANES), F32)
    else:
        k_ctx, v_ctx, bias, c0, m0 = ctx
        att = _neighborhood_attention(naqkv, k_ctx, v_ctx, bias, batch, seq_len)
    hf, hb, c_new, m_new = _bidir_mlstm(mqv, mkt, ift, c0, m0, batch, seq_len)
    x = _merge(x, mod, att, hf, hb, cvb, u, lw["merge_consts"], seq_len=seq_len)
    x = _ffn(x, mod, lw["g_pre_ffn"], lw["g_post_ffn"], lw["wg"], lw["wu"], lw["wd"], seq_len=seq_len)
    return x, (naqkv, c_new, m_new)


def kernel(x_prompt, x_sample, c, cache_k, cache_v, state_C, state_n, state_m, c_ctx, w_ada, b_ada, g_pre_mix,
           g_post_mix, g_pre_ffn, g_post_ffn, w_in, b_gate, m_b_i, m_b_f, na_rpb, m_norm_g, conv_w, w_br_na,
           w_br_m, w_br_cv, w_out, w_ffn_gate, w_ffn_up, w_ffn_down):
    batch, seq, d = x_prompt.shape
    dec_batch, dec_seq, _ = x_sample.shape
    depth = w_in.shape[0]
    past = cache_k.shape[3]
    assert dec_batch + 1 <= COND_ROWS

    cond = jnp.zeros((COND_ROWS, d), F32).at[:dec_batch].set(c).at[dec_batch].set(c_ctx)
    mod = _ada_modulation(cond, w_ada, b_ada).reshape(depth, COND_ROWS, N_MOD, d)

    xp = x_prompt.reshape(batch * seq, d)
    xs = x_sample.reshape(dec_batch * dec_seq, d)
    ks, vs, cs, ns, ms = [], [], [], [], []
    for l in range(depth):
        lw = _layer_weights(l, w_in, b_gate, m_b_i, m_b_f, g_pre_mix, g_post_mix, g_pre_ffn, g_post_ffn,
                            m_norm_g, conv_w, w_br_na, w_br_m, w_br_cv, w_out, w_ffn_gate, w_ffn_up, w_ffn_down)
        xp, (naqkv, c_l, m_l) = _trunk_layer(xp, mod[l, dec_batch:dec_batch + 1], lw, batch, seq, None)
        heads = lambda a: a.reshape(batch, seq, NA_HEADS, NA_HEAD_DIM).transpose(0, 2, 1, 3)
        ks.append(heads(naqkv[:, NA_WIDTH:2 * NA_WIDTH]))
        vs.append(heads(naqkv[:, 2 * NA_WIDTH:]))
        cs.append(c_l[..., :M_DV].reshape(batch, 2, M_HEADS, M_DK, M_DV))
        ns.append(c_l[..., M_DV].reshape(batch, 2, M_HEADS, M_DK))
        ms.append(m_l[:, :, 0].reshape(batch, 2, M_HEADS))

        tokens = lambda a: a.transpose(0, 2, 1, 3).reshape(dec_batch, past, NA_WIDTH).astype(BF16)
        c0 = jnp.concatenate([state_C[:, l], jnp.broadcast_to(state_n[:, l][..., None], state_C[:, l].shape)],
                             axis=-1).reshape(dec_batch, 2 * M_HEADS, M_DK, 2 * M_DV)
        m0 = jnp.broadcast_to(state_m[:, l].reshape(dec_batch, 2 * M_HEADS, 1), (dec_batch, 2 * M_HEADS, LANES))
        ctx = (tokens(cache_k[:, l]), tokens(cache_v[:, l]), _na_bias_tables(na_rpb[l], dec_seq // GRID_W), c0, m0)
        xs, _ = _trunk_layer(xs, mod[l, :dec_batch], lw, dec_batch, dec_seq, ctx)

    return (xp.reshape(batch, seq, d), xs.reshape(dec_batch, dec_seq, d),
            jnp.stack(ks, axis=1), jnp.stack(vs, axis=1), jnp.stack(cs, axis=1),
            jnp.stack(ns, axis=1), jnp.stack(ms, axis=1))
```

```python
import functools

import numpy as np
import jax
import jax.numpy as jnp
from jax import lax
from jax.experimental import pallas as pl
from jax.experimental.pallas import tpu as pltpu

F32 = jnp.float32
BF16 = jnp.bfloat16

GRID_W = 64
NA_HEADS = 8
NA_HEAD_DIM = 64
NA_WIDTH = NA_HEADS * NA_HEAD_DIM
NA_KH = 8
NA_KW = 16
NA_SCALE = NA_HEAD_DIM ** -0.5
M_HEADS = 4
M_DK = 128
M_DV = 128
M_WIDTH = M_HEADS * M_DV
M_CHUNK = 128
M_SCALE = M_DK ** -0.5
CONV_WIDTH = 512
N_BRANCH = 3
N_MOD = 6
EPS = 1e-6
NEG = -1e30

LANES = 128
SUBLANES = 8
VMEM_LIMIT_BYTES = 56 * 1024 * 1024

TOKEN_TILE = 512
NA_ROWS_PER_BLOCK = 4
M_CHUNKS_PER_STEP = 4
FFN_CHUNKS = 1
ADA_TILE = 1024
COND_ROWS = 16


def _const_spec(shape):
    nd = len(shape)
    return pl.BlockSpec(shape, lambda *_: (0,) * nd, pipeline_mode=pl.Buffered(1))


def _params(*sem):
    return pltpu.CompilerParams(dimension_semantics=sem, vmem_limit_bytes=VMEM_LIMIT_BYTES)


def _sigmoid(x):
    return 1.0 / (1.0 + jnp.exp(-x))


def _log_sigmoid(x):
    return jnp.minimum(x, 0.0) - jnp.log1p(jnp.exp(-jnp.abs(x)))


def _rmsnorm(x, g):
    return x * lax.rsqrt(jnp.mean(x * x, axis=-1, keepdims=True) + EPS) * g


def _dot(a, b):
    return jnp.dot(a, b, preferred_element_type=F32)


def _dot_nt(a, b):
    return lax.dot_general(a, b, (((1,), (1,)), ((), ())), preferred_element_type=F32)


def _dot_tn(a, b):
    return lax.dot_general(a, b, (((0,), (0,)), ((), ())), preferred_element_type=F32)


def _ada_kernel(cond_ref, w_ref, b_ref, o_ref):
    c = cond_ref[...]
    s = (c * _sigmoid(c)).astype(BF16)
    o_ref[0] = _dot(s, w_ref[0].astype(BF16)) + b_ref[0]


def _ada_modulation(cond, w_ada, b_ada):
    depth, d, n = w_ada.shape
    return pl.pallas_call(
        _ada_kernel,
        grid=(depth, n // ADA_TILE),
        in_specs=[
            pl.BlockSpec((COND_ROWS, d), lambda l, j: (0, 0)),
            pl.BlockSpec((1, d, ADA_TILE), lambda l, j: (l, 0, j)),
            pl.BlockSpec((1, 1, ADA_TILE), lambda l, j: (l, 0, j)),
        ],
        out_specs=pl.BlockSpec((1, COND_ROWS, ADA_TILE), lambda l, j: (l, 0, j)),
        out_shape=jax.ShapeDtypeStruct((depth, COND_ROWS, n), F32),
        compiler_params=_params("arbitrary", "arbitrary"),
        name="ada_modulation",
    )(cond, w_ada, b_ada.reshape(depth, 1, n))


def _mod_index_map(per_seq, tiles_per_seq):
    if per_seq:
        return lambda i: (i // tiles_per_seq, 0, 0)
    return lambda i: (0, 0, 0)


def _inproj_kernel(x_ref, mod_ref, g_ref, wa_ref, wkt_ref, wift_ref, bift_ref, wcv_ref,
                   naqkv_ref, mqv_ref, mkt_ref, ift_ref, cvb_ref, u_ref):
    mod = mod_ref[0]
    h = _rmsnorm(x_ref[...], g_ref[...]) * (1.0 + mod[1:2]) + mod[0:1]
    hb = h.astype(BF16)
    w = NA_WIDTH
    naqkv_ref[:, 0:w] = (_dot(hb, wa_ref[:, 0:w]) * NA_SCALE).astype(naqkv_ref.dtype)
    naqkv_ref[:, w:2 * w] = _dot(hb, wa_ref[:, w:2 * w]).astype(naqkv_ref.dtype)
    naqkv_ref[:, 2 * w:3 * w] = _dot(hb, wa_ref[:, 2 * w:3 * w]).astype(naqkv_ref.dtype)
    o = 3 * w
    m = M_WIDTH
    mqv_ref[:, 0:m] = (_dot(hb, wa_ref[:, o:o + m]) * M_SCALE).astype(BF16)
    mqv_ref[:, m:2 * m] = _dot(hb, wa_ref[:, o + m:o + 2 * m]).astype(BF16)
    mkt_ref[...] = _dot_nt(wkt_ref[...], hb).astype(BF16)
    ift_ref[...] = _dot_nt(wift_ref[...], hb) + bift_ref[...]
    c = CONV_WIDTH
    cvb_ref[...] = _dot(hb, wcv_ref[:, 0:c])
    u_ref[...] = _dot(hb, wcv_ref[:, c:2 * c]) * _dot(hb, wcv_ref[:, 2 * c:3 * c])


def _inproj(x, mod, g_pre, wa, wkt, wift, bift, wcv, *, seq_len, na_dtype):
    ntok, d = x.shape
    tile = min(TOKEN_TILE, ntok)
    per_seq = mod.shape[0] > 1
    assert ntok % tile == 0 and (not per_seq or seq_len % tile == 0)
    row = lambda n: pl.BlockSpec((tile, n), lambda i: (i, 0))
    col = lambda n: pl.BlockSpec((n, tile), lambda i: (0, i))
    n_gate = 4 * M_HEADS
    return pl.pallas_call(
        _inproj_kernel,
        grid=(ntok // tile,),
        in_specs=[
            row(d),
            pl.BlockSpec((1, N_MOD, d), _mod_index_map(per_seq, max(seq_len // tile, 1))),
            _const_spec(g_pre.shape),
            _const_spec(wa.shape),
            _const_spec(wkt.shape),
            _const_spec(wift.shape),
            _const_spec(bift.shape),
            _const_spec(wcv.shape),
        ],
        out_specs=[row(3 * NA_WIDTH), row(2 * M_WIDTH), col(M_WIDTH), col(n_gate),
                   row(CONV_WIDTH), row(CONV_WIDTH)],
        out_shape=[
            jax.ShapeDtypeStruct((ntok, 3 * NA_WIDTH), na_dtype),
            jax.ShapeDtypeStruct((ntok, 2 * M_WIDTH), BF16),
            jax.ShapeDtypeStruct((M_WIDTH, ntok), BF16),
            jax.ShapeDtypeStruct((n_gate, ntok), F32),
            jax.ShapeDtypeStruct((ntok, CONV_WIDTH), F32),
            jax.ShapeDtypeStruct((ntok, CONV_WIDTH), F32),
        ],
        compiler_params=_params("parallel"),
        name="in_projection",
    )(x, mod, g_pre, wa, wkt, wift, bift, wcv)


def _softmax_pv(scores, values):
    mx = functools.reduce(jnp.maximum, [jnp.max(s, axis=-1, keepdims=True) for s in scores])
    es = [jnp.exp(s - mx) for s in scores]
    den = functools.reduce(jnp.add, [jnp.sum(e, axis=-1, keepdims=True) for e in es])
    acc = functools.reduce(jnp.add, [_dot(e.astype(BF16), v) for e, v in zip(es, values)])
    return acc / den


def _ctx_attn_kernel(q_ref, k_ref, v_ref, o_ref):
    low = lax.broadcasted_iota(jnp.int32, (1, LANES), 1) < NA_HEAD_DIM
    for p in range(NA_WIDTH // LANES):
        sl = slice(p * LANES, (p + 1) * LANES)
        q2 = q_ref[0, :, sl].astype(BF16)
        k2 = k_ref[0, :, sl].astype(BF16)
        v2 = v_ref[0, :, sl].astype(BF16)
        halves = []
        for first in (True, False):
            qm = jnp.where(low == first, q2, jnp.zeros_like(q2))
            halves.append(_softmax_pv([_dot_nt(qm, k2)], [v2]))
        o_ref[0, :, sl] = jnp.where(low, halves[0], halves[1]).astype(o_ref.dtype)


def _context_attention(naqkv, batch, seq):
    a = naqkv.reshape(batch, seq, 3 * NA_WIDTH)
    spec = lambda j: pl.BlockSpec((1, seq, NA_WIDTH), lambda b: (b, 0, j))
    out = pl.pallas_call(
        _ctx_attn_kernel,
        grid=(batch,),
        in_specs=[spec(0), spec(1), spec(2)],
        out_specs=spec(0),
        out_shape=jax.ShapeDtypeStruct((batch, seq, NA_WIDTH), BF16),
        compiler_params=_params("parallel"),
        name="context_attention",
    )(a, a, a)
    return out.reshape(batch * seq, NA_WIDTH)


def _na_geometry(rows):
    r = NA_ROWS_PER_BLOCK
    slab = r + NA_KH - 1
    assert rows % r == 0 and rows >= slab and rows >= NA_KH
    nblk = rows // r
    pats = []
    for i in range(nblk):
        r0 = r * i
        s0 = int(np.clip(r0 - NA_KH // 2, 0, rows - slab))
        q_rows = r0 + np.arange(r)
        k_rows = s0 + np.arange(slab)
        rs = np.clip(q_rows - NA_KH // 2, 0, rows - NA_KH)
        valid = (k_rows[None, :] >= rs[:, None]) & (k_rows[None, :] < rs[:, None] + NA_KH)
        dr = np.clip(k_rows[None, :] - q_rows[:, None] + NA_KH - 1, 0, 2 * NA_KH - 2)
        pats.append((valid, dr))
    same = lambda a, b: np.array_equal(a[0], b[0]) and np.array_equal(a[1][a[0]], b[1][b[0]])
    assert nblk >= 3 and all(same(pats[1], p) for p in pats[1:-1])
    return r, slab, nblk, [pats[0], pats[1], pats[-1]]


def _na_bias_tables(rpb, rows):
    r, slab, _, pats = _na_geometry(rows)
    c = np.arange(GRID_W)
    cs = np.clip(c - NA_KW // 2, 0, GRID_W - NA_KW)
    kc = np.arange(GRID_W)
    col_ok = (kc[None, :] >= cs[:, None]) & (kc[None, :] < cs[:, None] + NA_KW)
    dc = np.clip(kc[None, :] - c[:, None] + NA_KW - 1, 0, 2 * NA_KW - 2)
    t4 = jnp.where(col_ok[None, None], rpb.astype(F32)[:, :, dc], NEG)
    out = []
    for valid, dr in pats:
        b = t4[:, dr]
        b = jnp.where(valid[None, :, :, None, None], b, NEG)
        out.append(b.transpose(0, 1, 3, 2, 4).reshape(rpb.shape[0], r * GRID_W, slab * GRID_W))
    return jnp.stack(out)


def _na_kernel(q_ref, k_ref, v_ref, kc_ref, vc_ref, bias_ref, o_ref, *, rows):
    r, slab, _, _ = _na_geometry(rows)
    i = pl.program_id(1)
    s0 = jnp.clip(r * i - NA_KH // 2, 0, rows - slab)
    start = pl.multiple_of(s0 * GRID_W, GRID_W)
    n_slab = slab * GRID_W
    low = lax.broadcasted_iota(jnp.int32, (1, LANES), 1) < NA_HEAD_DIM
    n_pairs = NA_WIDTH // LANES
    lanes = lambda p: slice(p * LANES, (p + 1) * LANES)

    def scores(head):
        p, half = divmod(head, 2)
        q2 = q_ref[0, :, lanes(p)]
        qm = jnp.where(low == (half == 0), q2, jnp.zeros_like(q2))
        s_loc = _dot_nt(qm, k_ref[0, pl.ds(start, n_slab), lanes(p)]) + bias_ref[0, head]
        s_ctx = _dot_nt(qm, kc_ref[0, :, lanes(p)])
        return s_loc, s_ctx

    def weights(s_loc, s_ctx):
        mx = jnp.maximum(jnp.max(s_loc, axis=-1, keepdims=True), jnp.max(s_ctx, axis=-1, keepdims=True))
        return jnp.exp(s_loc - mx).astype(BF16), jnp.exp(s_ctx - mx).astype(BF16)

    def attend(head, e_loc, e_ctx):
        p = head // 2
        ones = lambda n: jnp.ones((n, LANES), BF16)
        acc = (_dot(e_loc, jnp.concatenate([v_ref[0, pl.ds(start, n_slab), lanes(p)], ones(n_slab)], axis=-1))
               + _dot(e_ctx, jnp.concatenate([vc_ref[0, :, lanes(p)], ones(vc_ref.shape[1])], axis=-1)))
        return acc[:, :LANES] / acc[:, LANES:]

    sc, ex, out = {}, {}, {}
    for step in range(NA_HEADS + 2):
        if step < NA_HEADS:
            sc[step] = scores(step)
        if 0 <= step - 1 < NA_HEADS:
            ex[step - 1] = weights(*sc.pop(step - 1))
        if 0 <= step - 2 < NA_HEADS:
            head = step - 2
            out[head] = attend(head, *ex.pop(head))
            if head % 2 == 1:
                o_ref[0, :, lanes(head // 2)] = jnp.where(low, out.pop(head - 1), out.pop(head)).astype(o_ref.dtype)
    assert n_pairs * 2 == NA_HEADS


def _neighborhood_attention(naqkv, k_ctx, v_ctx, bias, batch, n):
    rows = n // GRID_W
    r, slab, nblk, _ = _na_geometry(rows)
    a = naqkv.reshape(batch, n, 3 * NA_WIDTH)
    past = k_ctx.shape[1]
    qb = r * GRID_W
    whole = lambda j: pl.BlockSpec((1, n, NA_WIDTH), lambda b, i: (b, 0, j))
    ctx = pl.BlockSpec((1, past, NA_WIDTH), lambda b, i: (b, 0, 0))
    pattern = lambda b, i: (jnp.where(i == 0, 0, jnp.where(i == nblk - 1, 2, 1)), 0, 0, 0)
    out = pl.pallas_call(
        functools.partial(_na_kernel, rows=rows),
        grid=(batch, nblk),
        in_specs=[
            pl.BlockSpec((1, qb, NA_WIDTH), lambda b, i: (b, i, 0)),
            whole(1), whole(2), ctx, ctx,
            pl.BlockSpec((1, NA_HEADS, qb, slab * GRID_W), pattern),
        ],
        out_specs=pl.BlockSpec((1, qb, NA_WIDTH), lambda b, i: (b, i, 0)),
        out_shape=jax.ShapeDtypeStruct((batch, n, NA_WIDTH), BF16),
        compiler_params=_params("parallel", "arbitrary"),
        name="neighborhood_attention",
    )(a, a, a, k_ctx, v_ctx, bias)
    return out.reshape(batch * n, NA_WIDTH)


def _mlstm_kernel(qf_ref, vf_ref, ktf_ref, iff_ref, qb_ref, vb_ref, ktb_ref, ifb_ref, c0_ref, m0_ref,
                  hf_ref, hb_ref, c_out_ref, m_out_ref, c_scr, m_scr):
    step = pl.program_id(1)
    length = M_CHUNK
    nh = M_HEADS

    @pl.when(step == 0)
    def _():
        c_scr[...] = c0_ref[0]
        m_scr[...] = m0_ref[0]

    t_idx = lax.broadcasted_iota(jnp.int32, (length, length), 0)
    s_idx = lax.broadcasted_iota(jnp.int32, (length, length), 1)
    ones = jnp.ones((length, M_DV), BF16)

    n_sub = qf_ref.shape[0] // length
    chains = []
    m_carry = [m_scr[j:j + 1, 0:1] for j in range(2 * nh)]
    for k in range(n_sub):
        for direction in range(2):
            q_ref, v_ref, kt_ref, if_ref = ((qf_ref, vf_ref, ktf_ref, iff_ref) if direction == 0
                                            else (qb_ref, vb_ref, ktb_ref, ifb_ref))
            sub = k if direction == 0 else n_sub - 1 - k
            tok = slice(sub * length, (sub + 1) * length)
            earlier = (s_idx <= t_idx) if direction == 0 else (s_idx >= t_idx)
            pre = if_ref[:, tok]
            log_f = _log_sigmoid(pre[2 * nh:4 * nh, :])
            tri = jnp.where((t_idx <= s_idx) if direction == 0 else (t_idx >= s_idx), 1.0, 0.0)
            b_rows = jnp.dot(log_f, tri, precision=lax.Precision.HIGHEST, preferred_element_type=F32)
            last = length - 1 if direction == 0 else 0
            for head in range(nh):
                j = direction * nh + head
                b_row = b_rows[j:j + 1, :]
                a_row = pre[j:j + 1, :] - b_row
                b_tot = b_row[:, last:last + 1]
                m_state = m_carry[j]
                g_row = b_tot + a_row
                m_new = jnp.maximum(b_tot + m_state, jnp.max(g_row, axis=-1, keepdims=True))
                m_carry[j] = m_new
                chains.append(dict(
                    k=k, j=j, sl=slice(head * M_DK, (head + 1) * M_DK), tok=tok,
                    q_ref=q_ref, v_ref=v_ref, kt_ref=kt_ref, earlier=earlier,
                    h_ref=hf_ref if direction == 0 else hb_ref,
                    f_row=log_f[j:j + 1, :], a_row=a_row, m_state=m_state,
                    decay=jnp.exp(b_tot + m_state - m_new), k_weight=jnp.exp(g_row - m_new)))

    for ch in chains:
        a_mask = jnp.where(ch["earlier"], ch["a_row"], NEG)
        m_col = jnp.maximum(ch["m_state"], jnp.max(a_mask, axis=-1, keepdims=True))
        b_col = jnp.sum(jnp.where(ch["earlier"], ch["f_row"], 0.0), axis=-1, keepdims=True)
        ch["a_mask"] = a_mask
        ch["m_rep"] = jnp.broadcast_to(m_col, (length, M_DV))
        ch["mt_rep"] = jnp.broadcast_to(b_col + m_col, (length, M_DV))
    for ch in chains:
        q = ch["q_ref"][ch["tok"], ch["sl"]]
        s = _dot(q, ch["kt_ref"][ch["sl"], ch["tok"]]) * jnp.exp(ch.pop("a_mask") - ch["m_rep"])
        ch["q"] = q
        ch["s"] = s.astype(BF16)
    for k in range(n_sub):
        group = [ch for ch in chains if ch["k"] == k]
        for ch in group:
            v_aug = jnp.concatenate([ch["v_ref"][ch["tok"], ch["sl"]], ones], axis=-1)
            w_int = jnp.exp(ch["m_state"] - ch["m_rep"])
            inter = _dot(ch.pop("q"), c_scr[ch["j"]].astype(BF16))
            intra = _dot(ch.pop("s"), v_aug)
            num = w_int * inter[:, :M_DV] + intra[:, :M_DV]
            den = w_int * inter[:, M_DV:] + intra[:, M_DV:]
            ch["h_ref"][ch["tok"], ch["sl"]] = num / jnp.maximum(jnp.abs(den), jnp.exp(-ch["mt_rep"]))
            ch["v_aug"] = v_aug
        for ch in group:
            j = ch["j"]
            kw = (ch["kt_ref"][ch["sl"], ch["tok"]].astype(F32) * ch["k_weight"]).astype(BF16)
            c_scr[j] = ch["decay"] * c_scr[j] + _dot(kw, ch.pop("v_aug"))
    for j in range(2 * nh):
        m_scr[j:j + 1, :] = jnp.broadcast_to(m_carry[j], (1, LANES))

    @pl.when(step == pl.num_programs(1) - 1)
    def _():
        c_out_ref[0] = c_scr[...]
        m_out_ref[0] = m_scr[...]


def _bidir_mlstm(mqv, mkt, ift, c0, m0, batch, n):
    span = min(M_CHUNKS_PER_STEP * M_CHUNK, n)
    assert n % span == 0 and span % M_CHUNK == 0
    nc = n // span
    nh2 = 2 * M_HEADS
    fwd = lambda b, c: b * nc + c
    bwd = lambda b, c: b * nc + nc - 1 - c
    tok = lambda j, at: pl.BlockSpec((span, M_WIDTH), lambda b, c: (at(b, c), j))
    feat = lambda rows, at: pl.BlockSpec((rows, span), lambda b, c: (0, at(b, c)))
    c_spec = pl.BlockSpec((1, nh2, M_DK, 2 * M_DV), lambda b, c: (b, 0, 0, 0))
    m_spec = pl.BlockSpec((1, nh2, LANES), lambda b, c: (b, 0, 0))
    stream = lambda at: [tok(0, at), tok(1, at), feat(M_WIDTH, at), feat(2 * nh2, at)]
    return pl.pallas_call(
        _mlstm_kernel,
        grid=(batch, nc),
        in_specs=stream(fwd) + stream(bwd) + [c_spec, m_spec],
        out_specs=[tok(0, fwd), tok(0, bwd), c_spec, m_spec],
        out_shape=[
            jax.ShapeDtypeStruct((batch * n, M_WIDTH), F32),
            jax.ShapeDtypeStruct((batch * n, M_WIDTH), F32),
            jax.ShapeDtypeStruct(c0.shape, F32),
            jax.ShapeDtypeStruct(m0.shape, F32),
        ],
        scratch_shapes=[
            pltpu.VMEM((nh2, M_DK, 2 * M_DV), F32),
            pltpu.VMEM((nh2, LANES), F32),
        ],
        compiler_params=_params("parallel", "arbitrary"),
        name="bidir_mlstm",
    )(mqv, mqv, mkt, ift, mqv, mqv, mkt, ift, c0, m0)


def _merge_kernel(x_ref, mod_ref, att_ref, hf_ref, hb_ref, cvb_ref, u_ref, uprev_ref, unext_ref,
                  gpre_ref, gpost_ref, mng_ref, convw_ref, bgate_ref,
                  wmo_ref, wgate_ref, wna_ref, wm_ref, wcv_ref, wout_ref,
                  o_ref, *, seq_len):
    tile, d = x_ref.shape
    x = x_ref[...]
    mod = mod_ref[0]
    h = _rmsnorm(x, gpre_ref[...]) * (1.0 + mod[1:2]) + mod[0:1]
    hb16 = h.astype(BF16)

    m_o = _dot(hb16, wmo_ref[...])
    hm_parts = []
    for head in range(M_HEADS):
        sl = slice(head * M_DV, (head + 1) * M_DV)
        hh = hf_ref[:, sl] + hb_ref[:, sl]
        hm_parts.append(hh * lax.rsqrt(jnp.mean(hh * hh, axis=-1, keepdims=True) + EPS))
    hm = jnp.concatenate(hm_parts, axis=-1) * mng_ref[...] * _sigmoid(m_o)

    u = u_ref[...]
    row = lax.broadcasted_iota(jnp.int32, (tile, 1), 0)
    pos = (pl.program_id(0) * tile + row) % seq_len
    u_before = jnp.where(row == 0, uprev_ref[SUBLANES - 1:SUBLANES, :], pltpu.roll(u, 1, 0))
    u_after = jnp.where(row == tile - 1, unext_ref[0:1, :], pltpu.roll(u, tile - 1, 0))
    u_before = jnp.where(pos == 0, 0.0, u_before)
    u_after = jnp.where(pos == seq_len - 1, 0.0, u_after)
    cw = convw_ref[...]
    hc = cvb_ref[...] * (u_before * cw[0:1] + u * cw[1:2] + u_after * cw[2:3])

    bg = bgate_ref[...]
    merged = (_sigmoid(_dot(hb16, wgate_ref[:, 0:d]) + bg[:, 0:d]) * _dot(att_ref[...], wna_ref[...])
              + _sigmoid(_dot(hb16, wgate_ref[:, d:2 * d]) + bg[:, d:2 * d]) * _dot(hm.astype(BF16), wm_ref[...])
              + _sigmoid(_dot(hb16, wgate_ref[:, 2 * d:3 * d]) + bg[:, 2 * d:3 * d])
              * _dot(hc.astype(BF16), wcv_ref[...]))
    y = _dot(merged.astype(BF16), wout_ref[...])
    o_ref[...] = x + mod[2:3] * _rmsnorm(y, gpost_ref[...])


def _merge(x, mod, att, hf, hb, cvb, u, consts, *, seq_len):
    ntok, d = x.shape
    tile = min(TOKEN_TILE, ntok)
    per_seq = mod.shape[0] > 1
    assert ntok % tile == 0 and (not per_seq or seq_len % tile == 0)
    assert tile % seq_len == 0 or seq_len % tile == 0
    row = lambda n: pl.BlockSpec((tile, n), lambda i: (i, 0))
    halo = tile // SUBLANES
    last_halo = ntok // SUBLANES - 1
    return pl.pallas_call(
        functools.partial(_merge_kernel, seq_len=seq_len),
        grid=(ntok // tile,),
        in_specs=[
            row(d),
            pl.BlockSpec((1, N_MOD, d), _mod_index_map(per_seq, max(seq_len // tile, 1))),
            row(NA_WIDTH), row(M_WIDTH), row(M_WIDTH), row(CONV_WIDTH), row(CONV_WIDTH),
            pl.BlockSpec((SUBLANES, CONV_WIDTH), lambda i: (jnp.maximum(i * halo - 1, 0), 0)),
            pl.BlockSpec((SUBLANES, CONV_WIDTH), lambda i: (jnp.minimum((i + 1) * halo, last_halo), 0)),
        ] + [_const_spec(c.shape) for c in consts],
        out_specs=row(d),
        out_shape=jax.ShapeDtypeStruct((ntok, d), F32),
        compiler_params=_params("parallel"),
        name="branch_merge",
    )(x, mod, att, hf, hb, cvb, u, u, u, *consts)


def _ffn_kernel(x_ref, mod_ref, gpre_ref, gpost_ref, wg_ref, wu_ref, wd_ref, o_ref):
    x = x_ref[...]
    mod = mod_ref[0]
    h = _rmsnorm(x, gpre_ref[...]) * (1.0 + mod[4:5]) + mod[3:4]
    hb16 = h.astype(BF16)
    d_ff = wg_ref.shape[1]
    chunk = d_ff // FFN_CHUNKS
    ff = None
    for c in range(FFN_CHUNKS):
        sl = slice(c * chunk, (c + 1) * chunk)
        gate = _dot(hb16, wg_ref[:, sl])
        act = (gate * _sigmoid(gate) * _dot(hb16, wu_ref[:, sl])).astype(BF16)
        part = _dot(act, wd_ref[sl, :])
        ff = part if ff is None else ff + part
    o_ref[...] = x + mod[5:6] * _rmsnorm(ff, gpost_ref[...])


def _ffn(x, mod, g_pre, g_post, wg, wu, wd, *, seq_len):
    ntok, d = x.shape
    tile = min(TOKEN_TILE, ntok)
    per_seq = mod.shape[0] > 1
    assert ntok % tile == 0 and (not per_seq or seq_len % tile == 0)
    assert wg.shape[1] % (FFN_CHUNKS * LANES) == 0
    row = pl.BlockSpec((tile, d), lambda i: (i, 0))
    return pl.pallas_call(
        _ffn_kernel,
        grid=(ntok // tile,),
        in_specs=[row, pl.BlockSpec((1, N_MOD, d), _mod_index_map(per_seq, max(seq_len // tile, 1))),
                  _const_spec(g_pre.shape), _const_spec(g_post.shape),
                  _const_spec(wg.shape), _const_spec(wu.shape), _const_spec(wd.shape)],
        out_specs=row,
        out_shape=jax.ShapeDtypeStruct((ntok, d), F32),
        compiler_params=_params("parallel"),
        name="swiglu_ffn",
    )(x, mod, g_pre, g_post, wg, wu, wd)


def _layer_weights(l, w_in, b_gate, m_b_i, m_b_f, g_pre_mix, g_post_mix, g_pre_ffn, g_post_ffn, m_norm_g, conv_w,
                   w_br_na, w_br_m, w_br_cv, w_out, w_ffn_gate, w_ffn_up, w_ffn_down):
    w = w_in[l]
    d = w.shape[0]
    o_mo = 3 * NA_WIDTH + 3 * M_WIDTH
    o_if = o_mo + M_WIDTH
    o_cv = o_if + 4 * M_HEADS
    o_gate = o_cv + 3 * CONV_WIDTH
    assert w.shape[1] == o_gate + N_BRANCH * d
    o_mk = 3 * NA_WIDTH + M_WIDTH
    o_mv = o_mk + M_WIDTH
    bift = jnp.concatenate([m_b_i[l].reshape(-1), m_b_f[l].reshape(-1)]).reshape(4 * M_HEADS, 1)
    vec = lambda a: a[l].reshape(1, -1)
    return dict(
        wa=jnp.concatenate([w[:, :o_mk], w[:, o_mv:o_mo]], axis=1).astype(BF16),
        wkt=w[:, o_mk:o_mv].T.astype(BF16), wift=w[:, o_if:o_cv].T.astype(BF16), bift=bift,
        wcv=w[:, o_cv:o_gate].astype(BF16),
        g_pre_mix=vec(g_pre_mix), g_pre_ffn=vec(g_pre_ffn), g_post_ffn=vec(g_post_ffn),
        merge_consts=(vec(g_pre_mix), vec(g_post_mix), vec(m_norm_g), conv_w[l], vec(b_gate),
                      w[:, o_mo:o_if].astype(BF16), w[:, o_gate:].astype(BF16),
                      w_br_na[l].astype(BF16), w_br_m[l].astype(BF16), w_br_cv[l].astype(BF16),
                      w_out[l].astype(BF16)),
        wg=w_ffn_gate[l].astype(BF16), wu=w_ffn_up[l].astype(BF16), wd=w_ffn_down[l].astype(BF16),
    )


def _trunk_layer(x, mod, lw, batch, seq_len, ctx):
    is_ctx = ctx is None
    naqkv, mqv, mkt, ift, cvb, u = _inproj(x, mod, lw["g_pre_mix"], lw["wa"], lw["wkt"], lw["wift"], lw["bift"],
                                           lw["wcv"], seq_len=seq_len, na_dtype=F32 if is_ctx else BF16)
    if is_ctx:
        att = _context_attention(naqkv, batch, seq_len)
        c0 = jnp.zeros((batch, 2 * M_HEADS, M_DK, 2 * M_DV), F32)
        m0 = jnp.zeros((batch, 2 * M_HEADS, LANES), F32)
    else:
        k_ctx, v_ctx, bias, c0, m0 = ctx
        att = _neighborhood_attention(naqkv, k_ctx, v_ctx, bias, batch, seq_len)
    hf, hb, c_new, m_new = _bidir_mlstm(mqv, mkt, ift, c0, m0, batch, seq_len)
    x = _merge(x, mod, att, hf, hb, cvb, u, lw["merge_consts"], seq_len=seq_len)
    x = _ffn(x, mod, lw["g_pre_ffn"], lw["g_post_ffn"], lw["wg"], lw["wu"], lw["wd"], seq_len=seq_len)
    return x, (naqkv, c_new, m_new)


def kernel(x_prompt, x_sample, c, cache_k, cache_v, state_C, state_n, state_m, c_ctx, w_ada, b_ada, g_pre_mix,
           g_post_mix, g_pre_ffn, g_post_ffn, w_in, b_gate, m_b_i, m_b_f, na_rpb, m_norm_g, conv_w, w_br_na,
           w_br_m, w_br_cv, w_out, w_ffn_gate, w_ffn_up, w_ffn_down):
    batch, seq, d = x_prompt.shape
    dec_batch, dec_seq, _ = x_sample.shape
    depth = w_in.shape[0]
    past = cache_k.shape[3]
    assert dec_batch + 1 <= COND_ROWS

    cond = jnp.zeros((COND_ROWS, d), F32).at[:dec_batch].set(c).at[dec_batch].set(c_ctx)
    mod = _ada_modulation(cond, w_ada, b_ada).reshape(depth, COND_ROWS, N_MOD, d)

    xp = x_prompt.reshape(batch * seq, d)
    xs = x_sample.reshape(dec_batch * dec_seq, d)
    ks, vs, cs, ns, ms = [], [], [], [], []
    for l in range(depth):
        lw = _layer_weights(l, w_in, b_gate, m_b_i, m_b_f, g_pre_mix, g_post_mix, g_pre_ffn, g_post_ffn,
                            m_norm_g, conv_w, w_br_na, w_br_m, w_br_cv, w_out, w_ffn_gate, w_ffn_up, w_ffn_down)
        xp, (naqkv, c_l, m_l) = _trunk_layer(xp, mod[l, dec_batch:dec_batch + 1], lw, batch, seq, None)
        heads = lambda a: a.reshape(batch, seq, NA_HEADS, NA_HEAD_DIM).transpose(0, 2, 1, 3)
        ks.append(heads(naqkv[:, NA_WIDTH:2 * NA_WIDTH]))
        vs.append(heads(naqkv[:, 2 * NA_WIDTH:]))
        cs.append(c_l[..., :M_DV].reshape(batch, 2, M_HEADS, M_DK, M_DV))
        ns.append(c_l[..., M_DV].reshape(batch, 2, M_HEADS, M_DK))
        ms.append(m_l[:, :, 0].reshape(batch, 2, M_HEADS))

        tokens = lambda a: a.transpose(0, 2, 1, 3).reshape(dec_batch, past, NA_WIDTH).astype(BF16)
        c0 = jnp.concatenate([state_C[:, l], jnp.broadcast_to(state_n[:, l][..., None], state_C[:, l].shape)],
                             axis=-1).reshape(dec_batch, 2 * M_HEADS, M_DK, 2 * M_DV)
        m0 = jnp.broadcast_to(state_m[:, l].reshape(dec_batch, 2 * M_HEADS, 1), (dec_batch, 2 * M_HEADS, LANES))
        ctx = (tokens(cache_k[:, l]), tokens(cache_v[:, l]), _na_bias_tables(na_rpb[l], dec_seq // GRID_W), c0, m0)
        xs, _ = _trunk_layer(xs, mod[l, :dec_batch], lw, dec_batch, dec_seq, ctx)

    return (xp.reshape(batch, seq, d), xs.reshape(dec_batch, dec_seq, d),
            jnp.stack(ks, axis=1), jnp.stack(vs, axis=1), jnp.stack(cs, axis=1),
            jnp.stack(ns, axis=1), jnp.stack(ms, axis=1))
```

```python
import functools

import numpy as np
import jax
import jax.numpy as jnp
from jax import lax
from jax.experimental import pallas as pl
from jax.experimental.pallas import tpu as pltpu

F32 = jnp.float32
BF16 = jnp.bfloat16

GRID_W = 64
NA_HEADS = 8
NA_HEAD_DIM = 64
NA_WIDTH = NA_HEADS * NA_HEAD_DIM
NA_KH = 8
NA_KW = 16
NA_SCALE = NA_HEAD_DIM ** -0.5
M_HEADS = 4
M_DK = 128
M_DV = 128
M_WIDTH = M_HEADS * M_DV
M_CHUNK = 128
M_SCALE = M_DK ** -0.5
CONV_WIDTH = 512
N_BRANCH = 3
N_MOD = 6
EPS = 1e-6
NEG = -1e30

LANES = 128
SUBLANES = 8
VMEM_LIMIT_BYTES = 56 * 1024 * 1024

TOKEN_TILE = 512
NA_ROWS_PER_BLOCK = 4
M_CHUNKS_PER_STEP = 4
FFN_CHUNKS = 1
ADA_TILE = 1024
COND_ROWS = 16


def _const_spec(shape):
    nd = len(shape)
    return pl.BlockSpec(shape, lambda *_: (0,) * nd, pipeline_mode=pl.Buffered(1))


def _resident(w):
    if isinstance(w, tuple):
        stacked, layer = w
        nd = stacked.ndim - 1
        return stacked, pl.BlockSpec((None,) + stacked.shape[1:], lambda *_: (layer,) + (0,) * nd,
                                     pipeline_mode=pl.Buffered(1))
    return w, _const_spec(w.shape)


def _params(*sem):
    return pltpu.CompilerParams(dimension_semantics=sem, vmem_limit_bytes=VMEM_LIMIT_BYTES)


def _sigmoid(x):
    return 1.0 / (1.0 + jnp.exp(-x))


def _log_sigmoid(x):
    return jnp.minimum(x, 0.0) - jnp.log1p(jnp.exp(-jnp.abs(x)))


def _rmsnorm(x, g):
    return x * lax.rsqrt(jnp.mean(x * x, axis=-1, keepdims=True) + EPS) * g


def _dot(a, b):
    return jnp.dot(a, b, preferred_element_type=F32)


def _dot_nt(a, b):
    return lax.dot_general(a, b, (((1,), (1,)), ((), ())), preferred_element_type=F32)


def _dot_tn(a, b):
    return lax.dot_general(a, b, (((0,), (0,)), ((), ())), preferred_element_type=F32)


def _ada_kernel(cond_ref, w_ref, b_ref, o_ref):
    c = cond_ref[...]
    s = (c * _sigmoid(c)).astype(BF16)
    o_ref[0] = _dot(s, w_ref[0].astype(BF16)) + b_ref[0]


def _ada_modulation(cond, w_ada, b_ada):
    depth, d, n = w_ada.shape
    return pl.pallas_call(
        _ada_kernel,
        grid=(depth, n // ADA_TILE),
        in_specs=[
            pl.BlockSpec((COND_ROWS, d), lambda l, j: (0, 0)),
            pl.BlockSpec((1, d, ADA_TILE), lambda l, j: (l, 0, j)),
            pl.BlockSpec((1, 1, ADA_TILE), lambda l, j: (l, 0, j)),
        ],
        out_specs=pl.BlockSpec((1, COND_ROWS, ADA_TILE), lambda l, j: (l, 0, j)),
        out_shape=jax.ShapeDtypeStruct((depth, COND_ROWS, n), F32),
        compiler_params=_params("arbitrary", "arbitrary"),
        name="ada_modulation",
    )(cond, w_ada, b_ada.reshape(depth, 1, n))


def _mod_index_map(per_seq, tiles_per_seq):
    if per_seq:
        return lambda i: (i // tiles_per_seq, 0, 0)
    return lambda i: (0, 0, 0)


def _inproj_kernel(x_ref, mod_ref, g_ref, wa_ref, wkt_ref, wift_ref, bift_ref, wcv_ref,
                   naqkv_ref, mqv_ref, mkt_ref, ift_ref, cvb_ref, u_ref):
    mod = mod_ref[0]
    h = _rmsnorm(x_ref[...], g_ref[...]) * (1.0 + mod[1:2]) + mod[0:1]
    hb = h.astype(BF16)
    w = NA_WIDTH
    naqkv_ref[:, 0:w] = (_dot(hb, wa_ref[:, 0:w]) * NA_SCALE).astype(naqkv_ref.dtype)
    naqkv_ref[:, w:2 * w] = _dot(hb, wa_ref[:, w:2 * w]).astype(naqkv_ref.dtype)
    naqkv_ref[:, 2 * w:3 * w] = _dot(hb, wa_ref[:, 2 * w:3 * w]).astype(naqkv_ref.dtype)
    o = 3 * w
    m = M_WIDTH
    mqv_ref[:, 0:m] = (_dot(hb, wa_ref[:, o:o + m]) * M_SCALE).astype(BF16)
    mqv_ref[:, m:2 * m] = _dot(hb, wa_ref[:, o + m:o + 2 * m]).astype(BF16)
    mkt_ref[...] = _dot_nt(wkt_ref[...], hb).astype(BF16)
    ift_ref[...] = _dot_nt(wift_ref[...], hb) + bift_ref[...]
    c = CONV_WIDTH
    cvb_ref[...] = _dot(hb, wcv_ref[:, 0:c])
    u_ref[...] = _dot(hb, wcv_ref[:, c:2 * c]) * _dot(hb, wcv_ref[:, 2 * c:3 * c])


def _inproj(x, mod, g_pre, wa, wkt, wift, bift, wcv, *, seq_len, na_dtype):
    ntok, d = x.shape
    tile = min(TOKEN_TILE, ntok)
    per_seq = mod.shape[0] > 1
    assert ntok % tile == 0 and (not per_seq or seq_len % tile == 0)
    row = lambda n: pl.BlockSpec((tile, n), lambda i: (i, 0))
    col = lambda n: pl.BlockSpec((n, tile), lambda i: (0, i))
    n_gate = 4 * M_HEADS
    return pl.pallas_call(
        _inproj_kernel,
        grid=(ntok // tile,),
        in_specs=[
            row(d),
            pl.BlockSpec((1, N_MOD, d), _mod_index_map(per_seq, max(seq_len // tile, 1))),
            _const_spec(g_pre.shape),
            _const_spec(wa.shape),
            _const_spec(wkt.shape),
            _const_spec(wift.shape),
            _const_spec(bift.shape),
            _const_spec(wcv.shape),
        ],
        out_specs=[row(3 * NA_WIDTH), row(2 * M_WIDTH), col(M_WIDTH), col(n_gate),
                   row(CONV_WIDTH), row(CONV_WIDTH)],
        out_shape=[
            jax.ShapeDtypeStruct((ntok, 3 * NA_WIDTH), na_dtype),
            jax.ShapeDtypeStruct((ntok, 2 * M_WIDTH), BF16),
            jax.ShapeDtypeStruct((M_WIDTH, ntok), BF16),
            jax.ShapeDtypeStruct((n_gate, ntok), F32),
            jax.ShapeDtypeStruct((ntok, CONV_WIDTH), F32),
            jax.ShapeDtypeStruct((ntok, CONV_WIDTH), F32),
        ],
        compiler_params=_params("parallel"),
        name="in_projection",
    )(x, mod, g_pre, wa, wkt, wift, bift, wcv)


def _softmax_pv(scores, values):
    mx = functools.reduce(jnp.maximum, [jnp.max(s, axis=-1, keepdims=True) for s in scores])
    es = [jnp.exp(s - mx) for s in scores]
    den = functools.reduce(jnp.add, [jnp.sum(e, axis=-1, keepdims=True) for e in es])
    acc = functools.reduce(jnp.add, [_dot(e.astype(BF16), v) for e, v in zip(es, values)])
    return acc / den


def _ctx_attn_kernel(q_ref, k_ref, v_ref, o_ref):
    low = lax.broadcasted_iota(jnp.int32, (1, LANES), 1) < NA_HEAD_DIM
    for p in range(NA_WIDTH // LANES):
        sl = slice(p * LANES, (p + 1) * LANES)
        q2 = q_ref[0, :, sl].astype(BF16)
        k2 = k_ref[0, :, sl].astype(BF16)
        v2 = v_ref[0, :, sl].astype(BF16)
        halves = []
        for first in (True, False):
            qm = jnp.where(low == first, q2, jnp.zeros_like(q2))
            halves.append(_softmax_pv([_dot_nt(qm, k2)], [v2]))
        o_ref[0, :, sl] = jnp.where(low, halves[0], halves[1]).astype(o_ref.dtype)


def _context_attention(naqkv, batch, seq):
    a = naqkv.reshape(batch, seq, 3 * NA_WIDTH)
    spec = lambda j: pl.BlockSpec((1, seq, NA_WIDTH), lambda b: (b, 0, j))
    out = pl.pallas_call(
        _ctx_attn_kernel,
        grid=(batch,),
        in_specs=[spec(0), spec(1), spec(2)],
        out_specs=spec(0),
        out_shape=jax.ShapeDtypeStruct((batch, seq, NA_WIDTH), BF16),
        compiler_params=_params("parallel"),
        name="context_attention",
    )(a, a, a)
    return out.reshape(batch * seq, NA_WIDTH)


def _na_geometry(rows):
    r = NA_ROWS_PER_BLOCK
    slab = r + NA_KH - 1
    assert rows % r == 0 and rows >= slab and rows >= NA_KH
    nblk = rows // r
    pats = []
    for i in range(nblk):
        r0 = r * i
        s0 = int(np.clip(r0 - NA_KH // 2, 0, rows - slab))
        q_rows = r0 + np.arange(r)
        k_rows = s0 + np.arange(slab)
        rs = np.clip(q_rows - NA_KH // 2, 0, rows - NA_KH)
        valid = (k_rows[None, :] >= rs[:, None]) & (k_rows[None, :] < rs[:, None] + NA_KH)
        dr = np.clip(k_rows[None, :] - q_rows[:, None] + NA_KH - 1, 0, 2 * NA_KH - 2)
        pats.append((valid, dr))
    same = lambda a, b: np.array_equal(a[0], b[0]) and np.array_equal(a[1][a[0]], b[1][b[0]])
    assert nblk >= 3 and all(same(pats[1], p) for p in pats[1:-1])
    return r, slab, nblk, [pats[0], pats[1], pats[-1]]


def _na_bias_tables(rpb, rows):
    r, slab, _, pats = _na_geometry(rows)
    c = np.arange(GRID_W)
    cs = np.clip(c - NA_KW // 2, 0, GRID_W - NA_KW)
    kc = np.arange(GRID_W)
    col_ok = (kc[None, :] >= cs[:, None]) & (kc[None, :] < cs[:, None] + NA_KW)
    dc = np.clip(kc[None, :] - c[:, None] + NA_KW - 1, 0, 2 * NA_KW - 2)
    t4 = jnp.where(col_ok[None, None], rpb.astype(F32)[:, :, dc], NEG)
    masked = jnp.full((rpb.shape[0], GRID_W, GRID_W), NEG, F32)
    out = []
    for valid, dr in pats:
        tile_rows = [jnp.concatenate([t4[:, dr[i, j]] if valid[i, j] else masked for j in range(slab)], axis=-1)
                     for i in range(r)]
        out.append(jnp.concatenate(tile_rows, axis=1))
    return jnp.stack(out)


def _na_kernel(q_ref, k_ref, v_ref, kc_ref, vc_ref, bias_ref, o_ref, *, rows):
    r, slab, _, _ = _na_geometry(rows)
    i = pl.program_id(1)
    s0 = jnp.clip(r * i - NA_KH // 2, 0, rows - slab)
    start = pl.multiple_of(s0 * GRID_W, GRID_W)
    n_slab = slab * GRID_W
    low = lax.broadcasted_iota(jnp.int32, (1, LANES), 1) < NA_HEAD_DIM
    n_pairs = NA_WIDTH // LANES
    lanes = lambda p: slice(p * LANES, (p + 1) * LANES)

    def scores(head):
        p, half = divmod(head, 2)
        q2 = q_ref[0, :, lanes(p)]
        qm = jnp.where(low == (half == 0), q2, jnp.zeros_like(q2))
        s_loc = _dot_nt(qm, k_ref[0, pl.ds(start, n_slab), lanes(p)]) + bias_ref[0, head]
        s_ctx = _dot_nt(qm, kc_ref[0, :, lanes(p)])
        return s_loc, s_ctx

    def weights(s_loc, s_ctx):
        mx = jnp.maximum(jnp.max(s_loc, axis=-1, keepdims=True), jnp.max(s_ctx, axis=-1, keepdims=True))
        return jnp.exp(s_loc - mx).astype(BF16), jnp.exp(s_ctx - mx).astype(BF16)

    def attend(head, e_loc, e_ctx):
        p = head // 2
        ones = lambda n: jnp.ones((n, LANES), BF16)
        acc = (_dot(e_loc, jnp.concatenate([v_ref[0, pl.ds(start, n_slab), lanes(p)], ones(n_slab)], axis=-1))
               + _dot(e_ctx, jnp.concatenate([vc_ref[0, :, lanes(p)], ones(vc_ref.shape[1])], axis=-1)))
        return acc[:, :LANES] / acc[:, LANES:]

    sc, ex, out = {}, {}, {}
    for step in range(NA_HEADS + 2):
        if step < NA_HEADS:
            sc[step] = scores(step)
        if 0 <= step - 1 < NA_HEADS:
            ex[step - 1] = weights(*sc.pop(step - 1))
        if 0 <= step - 2 < NA_HEADS:
            head = step - 2
            out[head] = attend(head, *ex.pop(head))
            if head % 2 == 1:
                o_ref[0, :, lanes(head // 2)] = jnp.where(low, out.pop(head - 1), out.pop(head)).astype(o_ref.dtype)
    assert n_pairs * 2 == NA_HEADS


def _neighborhood_attention(naqkv, k_ctx, v_ctx, bias, batch, n):
    rows = n // GRID_W
    r, slab, nblk, _ = _na_geometry(rows)
    a = naqkv.reshape(batch, n, 3 * NA_WIDTH)
    past = k_ctx.shape[1]
    qb = r * GRID_W
    whole = lambda j: pl.BlockSpec((1, n, NA_WIDTH), lambda b, i: (b, 0, j))
    ctx = pl.BlockSpec((1, past, NA_WIDTH), lambda b, i: (b, 0, 0))
    pattern = lambda b, i: (jnp.where(i == 0, 0, jnp.where(i == nblk - 1, 2, 1)), 0, 0, 0)
    out = pl.pallas_call(
        functools.partial(_na_kernel, rows=rows),
        grid=(batch, nblk),
        in_specs=[
            pl.BlockSpec((1, qb, NA_WIDTH), lambda b, i: (b, i, 0)),
            whole(1), whole(2), ctx, ctx,
            pl.BlockSpec((1, NA_HEADS, qb, slab * GRID_W), pattern),
        ],
        out_specs=pl.BlockSpec((1, qb, NA_WIDTH), lambda b, i: (b, i, 0)),
        out_shape=jax.ShapeDtypeStruct((batch, n, NA_WIDTH), BF16),
        compiler_params=_params("parallel", "arbitrary"),
        name="neighborhood_attention",
    )(a, a, a, k_ctx, v_ctx, bias)
    return out.reshape(batch * n, NA_WIDTH)


def _mlstm_kernel(qf_ref, vf_ref, ktf_ref, iff_ref, qb_ref, vb_ref, ktb_ref, ifb_ref, c0_ref, m0_ref,
                  hf_ref, hb_ref, c_out_ref, m_out_ref, c_scr, m_scr):
    step = pl.program_id(1)
    length = M_CHUNK
    nh = M_HEADS

    @pl.when(step == 0)
    def _():
        c_scr[...] = c0_ref[0]
        m_scr[...] = m0_ref[0]

    t_idx = lax.broadcasted_iota(jnp.int32, (length, length), 0)
    s_idx = lax.broadcasted_iota(jnp.int32, (length, length), 1)
    ones = jnp.ones((length, M_DV), BF16)

    n_sub = qf_ref.shape[0] // length
    chains = []
    m_carry = [m_scr[j:j + 1, 0:1] for j in range(2 * nh)]
    for k in range(n_sub):
        for direction in range(2):
            q_ref, v_ref, kt_ref, if_ref = ((qf_ref, vf_ref, ktf_ref, iff_ref) if direction == 0
                                            else (qb_ref, vb_ref, ktb_ref, ifb_ref))
            sub = k if direction == 0 else n_sub - 1 - k
            tok = slice(sub * length, (sub + 1) * length)
            earlier = (s_idx <= t_idx) if direction == 0 else (s_idx >= t_idx)
            pre = if_ref[:, tok]
            log_f = _log_sigmoid(pre[2 * nh:4 * nh, :])
            tri = jnp.where((t_idx <= s_idx) if direction == 0 else (t_idx >= s_idx), 1.0, 0.0)
            b_rows = jnp.dot(log_f, tri, precision=lax.Precision.HIGHEST, preferred_element_type=F32)
            last = length - 1 if direction == 0 else 0
            for head in range(nh):
                j = direction * nh + head
                b_row = b_rows[j:j + 1, :]
                a_row = pre[j:j + 1, :] - b_row
                b_tot = b_row[:, last:last + 1]
                m_state = m_carry[j]
                g_row = b_tot + a_row
                m_new = jnp.maximum(b_tot + m_state, jnp.max(g_row, axis=-1, keepdims=True))
                m_carry[j] = m_new
                chains.append(dict(
                    k=k, j=j, sl=slice(head * M_DK, (head + 1) * M_DK), tok=tok,
                    q_ref=q_ref, v_ref=v_ref, kt_ref=kt_ref, earlier=earlier,
                    h_ref=hf_ref if direction == 0 else hb_ref,
                    f_row=log_f[j:j + 1, :], a_row=a_row, m_state=m_state,
                    decay=jnp.exp(b_tot + m_state - m_new), k_weight=jnp.exp(g_row - m_new)))

    for ch in chains:
        a_mask = jnp.where(ch["earlier"], ch["a_row"], NEG)
        m_col = jnp.maximum(ch["m_state"], jnp.max(a_mask, axis=-1, keepdims=True))
        b_col = jnp.sum(jnp.where(ch["earlier"], ch["f_row"], 0.0), axis=-1, keepdims=True)
        ch["a_mask"] = a_mask
        ch["m_rep"] = jnp.broadcast_to(m_col, (length, M_DV))
        ch["mt_rep"] = jnp.broadcast_to(b_col + m_col, (length, M_DV))
    for ch in chains:
        q = ch["q_ref"][ch["tok"], ch["sl"]]
        s = _dot(q, ch["kt_ref"][ch["sl"], ch["tok"]]) * jnp.exp(ch.pop("a_mask") - ch["m_rep"])
        ch["q"] = q
        ch["s"] = s.astype(BF16)
    for k in range(n_sub):
        group = [ch for ch in chains if ch["k"] == k]
        for ch in group:
            v_aug = jnp.concatenate([ch["v_ref"][ch["tok"], ch["sl"]], ones], axis=-1)
            w_int = jnp.exp(ch["m_state"] - ch["m_rep"])
            inter = _dot(ch.pop("q"), c_scr[ch["j"]].astype(BF16))
            intra = _dot(ch.pop("s"), v_aug)
            num = w_int * inter[:, :M_DV] + intra[:, :M_DV]
            den = w_int * inter[:, M_DV:] + intra[:, M_DV:]
            ch["h_ref"][ch["tok"], ch["sl"]] = num / jnp.maximum(jnp.abs(den), jnp.exp(-ch["mt_rep"]))
            ch["v_aug"] = v_aug
        for ch in group:
            j = ch["j"]
            kw = (ch["kt_ref"][ch["sl"], ch["tok"]].astype(F32) * ch["k_weight"]).astype(BF16)
            c_scr[j] = ch["decay"] * c_scr[j] + _dot(kw, ch.pop("v_aug"))
    for j in range(2 * nh):
        m_scr[j:j + 1, :] = jnp.broadcast_to(m_carry[j], (1, LANES))

    @pl.when(step == pl.num_programs(1) - 1)
    def _():
        c_out_ref[0] = c_scr[...]
        m_out_ref[0] = m_scr[...]


def _bidir_mlstm(mqv, mkt, ift, c0, m0, batch, n):
    span = min(M_CHUNKS_PER_STEP * M_CHUNK, n)
    assert n % span == 0 and span % M_CHUNK == 0
    nc = n // span
    nh2 = 2 * M_HEADS
    fwd = lambda b, c: b * nc + c
    bwd = lambda b, c: b * nc + nc - 1 - c
    tok = lambda j, at: pl.BlockSpec((span, M_WIDTH), lambda b, c: (at(b, c), j))
    feat = lambda rows, at: pl.BlockSpec((rows, span), lambda b, c: (0, at(b, c)))
    c_spec = pl.BlockSpec((1, nh2, M_DK, 2 * M_DV), lambda b, c: (b, 0, 0, 0))
    m_spec = pl.BlockSpec((1, nh2, LANES), lambda b, c: (b, 0, 0))
    stream = lambda at: [tok(0, at), tok(1, at), feat(M_WIDTH, at), feat(2 * nh2, at)]
    return pl.pallas_call(
        _mlstm_kernel,
        grid=(batch, nc),
        in_specs=stream(fwd) + stream(bwd) + [c_spec, m_spec],
        out_specs=[tok(0, fwd), tok(0, bwd), c_spec, m_spec],
        out_shape=[
            jax.ShapeDtypeStruct((batch * n, M_WIDTH), F32),
            jax.ShapeDtypeStruct((batch * n, M_WIDTH), F32),
            jax.ShapeDtypeStruct(c0.shape, F32),
            jax.ShapeDtypeStruct(m0.shape, F32),
        ],
        scratch_shapes=[
            pltpu.VMEM((nh2, M_DK, 2 * M_DV), F32),
            pltpu.VMEM((nh2, LANES), F32),
        ],
        compiler_params=_params("parallel", "arbitrary"),
        name="bidir_mlstm",
    )(mqv, mqv, mkt, ift, mqv, mqv, mkt, ift, c0, m0)


def _merge_kernel(x_ref, mod_ref, att_ref, hf_ref, hb_ref, cvb_ref, u_ref, uprev_ref, unext_ref,
                  gpre_ref, gpost_ref, mng_ref, convw_ref, bgate_ref,
                  wmo_ref, wgate_ref, wna_ref, wm_ref, wcv_ref, wout_ref,
                  o_ref, *, seq_len):
    tile, d = x_ref.shape
    x = x_ref[...]
    mod = mod_ref[0]
    h = _rmsnorm(x, gpre_ref[...]) * (1.0 + mod[1:2]) + mod[0:1]
    hb16 = h.astype(BF16)

    m_o = _dot(hb16, wmo_ref[...])
    hm_parts = []
    for head in range(M_HEADS):
        sl = slice(head * M_DV, (head + 1) * M_DV)
        hh = hf_ref[:, sl] + hb_ref[:, sl]
        hm_parts.append(hh * lax.rsqrt(jnp.mean(hh * hh, axis=-1, keepdims=True) + EPS))
    hm = jnp.concatenate(hm_parts, axis=-1) * mng_ref[...] * _sigmoid(m_o)

    u = u_ref[...]
    row = lax.broadcasted_iota(jnp.int32, (tile, 1), 0)
    pos = (pl.program_id(0) * tile + row) % seq_len
    u_before = jnp.where(row == 0, uprev_ref[SUBLANES - 1:SUBLANES, :], pltpu.roll(u, 1, 0))
    u_after = jnp.where(row == tile - 1, unext_ref[0:1, :], pltpu.roll(u, tile - 1, 0))
    u_before = jnp.where(pos == 0, 0.0, u_before)
    u_after = jnp.where(pos == seq_len - 1, 0.0, u_after)
    cw = convw_ref[...]
    hc = cvb_ref[...] * (u_before * cw[0:1] + u * cw[1:2] + u_after * cw[2:3])

    bg = bgate_ref[...]
    merged = (_sigmoid(_dot(hb16, wgate_ref[:, 0:d]) + bg[:, 0:d]) * _dot(att_ref[...], wna_ref[...])
              + _sigmoid(_dot(hb16, wgate_ref[:, d:2 * d]) + bg[:, d:2 * d]) * _dot(hm.astype(BF16), wm_ref[...])
              + _sigmoid(_dot(hb16, wgate_ref[:, 2 * d:3 * d]) + bg[:, 2 * d:3 * d])
              * _dot(hc.astype(BF16), wcv_ref[...]))
    y = _dot(merged.astype(BF16), wout_ref[...])
    o_ref[...] = x + mod[2:3] * _rmsnorm(y, gpost_ref[...])


def _merge(x, mod, att, hf, hb, cvb, u, consts, *, seq_len):
    ntok, d = x.shape
    tile = min(TOKEN_TILE, ntok)
    per_seq = mod.shape[0] > 1
    assert ntok % tile == 0 and (not per_seq or seq_len % tile == 0)
    assert tile % seq_len == 0 or seq_len % tile == 0
    row = lambda n: pl.BlockSpec((tile, n), lambda i: (i, 0))
    halo = tile // SUBLANES
    last_halo = ntok // SUBLANES - 1
    const_ops, const_specs = zip(*[_resident(c) for c in consts])
    return pl.pallas_call(
        functools.partial(_merge_kernel, seq_len=seq_len),
        grid=(ntok // tile,),
        in_specs=[
            row(d),
            pl.BlockSpec((1, N_MOD, d), _mod_index_map(per_seq, max(seq_len // tile, 1))),
            row(NA_WIDTH), row(M_WIDTH), row(M_WIDTH), row(CONV_WIDTH), row(CONV_WIDTH),
            pl.BlockSpec((SUBLANES, CONV_WIDTH), lambda i: (jnp.maximum(i * halo - 1, 0), 0)),
            pl.BlockSpec((SUBLANES, CONV_WIDTH), lambda i: (jnp.minimum((i + 1) * halo, last_halo), 0)),
        ] + list(const_specs),
        out_specs=row(d),
        out_shape=jax.ShapeDtypeStruct((ntok, d), F32),
        compiler_params=_params("parallel"),
        name="branch_merge",
    )(x, mod, att, hf, hb, cvb, u, u, u, *const_ops)


def _ffn_kernel(x_ref, mod_ref, gpre_ref, gpost_ref, wg_ref, wu_ref, wd_ref, o_ref):
    x = x_ref[...]
    mod = mod_ref[0]
    h = _rmsnorm(x, gpre_ref[...]) * (1.0 + mod[4:5]) + mod[3:4]
    hb16 = h.astype(BF16)
    d_ff = wg_ref.shape[1]
    chunk = d_ff // FFN_CHUNKS
    ff = None
    for c in range(FFN_CHUNKS):
        sl = slice(c * chunk, (c + 1) * chunk)
        gate = _dot(hb16, wg_ref[:, sl])
        act = (gate * _sigmoid(gate) * _dot(hb16, wu_ref[:, sl])).astype(BF16)
        part = _dot(act, wd_ref[sl, :])
        ff = part if ff is None else ff + part
    o_ref[...] = x + mod[5:6] * _rmsnorm(ff, gpost_ref[...])


def _ffn(x, mod, g_pre, g_post, wg, wu, wd, *, seq_len):
    ntok, d = x.shape
    tile = min(TOKEN_TILE, ntok)
    per_seq = mod.shape[0] > 1
    assert ntok % tile == 0 and (not per_seq or seq_len % tile == 0)
    const_ops, const_specs = zip(*[_resident(c) for c in (g_pre, g_post, wg, wu, wd)])
    assert const_ops[2].shape[-1] % (FFN_CHUNKS * 2 * LANES) == 0
    row = pl.BlockSpec((tile, d), lambda i: (i, 0))
    return pl.pallas_call(
        _ffn_kernel,
        grid=(ntok // tile,),
        in_specs=[row, pl.BlockSpec((1, N_MOD, d), _mod_index_map(per_seq, max(seq_len // tile, 1)))]
        + list(const_specs),
        out_specs=row,
        out_shape=jax.ShapeDtypeStruct((ntok, d), F32),
        compiler_params=_params("parallel"),
        name="swiglu_ffn",
    )(x, mod, *const_ops)


def _layer_weights(l, w_in, b_gate, m_b_i, m_b_f, g_pre_mix, g_post_mix, g_pre_ffn, g_post_ffn, m_norm_g, conv_w,
                   w_br_na, w_br_m, w_br_cv, w_out, w_ffn_gate, w_ffn_up, w_ffn_down):
    d = w_in.shape[1]
    o_mo = 3 * NA_WIDTH + 3 * M_WIDTH
    o_if = o_mo + M_WIDTH
    o_cv = o_if + 4 * M_HEADS
    o_gate = o_cv + 3 * CONV_WIDTH
    assert w_in.shape[2] == o_gate + N_BRANCH * d
    o_mk = 3 * NA_WIDTH + M_WIDTH
    o_mv = o_mk + M_WIDTH
    cols = lambda a, b: w_in[l, :, a:b].astype(BF16)
    bift = jnp.concatenate([m_b_i[l].reshape(-1), m_b_f[l].reshape(-1)]).reshape(4 * M_HEADS, 1)
    vec = lambda a: a[l].reshape(1, -1)
    return dict(
        wa=jnp.concatenate([cols(0, o_mk), cols(o_mv, o_mo)], axis=1),
        wkt=cols(o_mk, o_mv).T, wift=cols(o_if, o_cv).T, bift=bift, wcv=cols(o_cv, o_gate),
        g_pre_mix=vec(g_pre_mix), g_pre_ffn=vec(g_pre_ffn), g_post_ffn=vec(g_post_ffn),
        merge_consts=(vec(g_pre_mix), vec(g_post_mix), vec(m_norm_g), conv_w[l], vec(b_gate),
                      cols(o_mo, o_if), cols(o_gate, o_gate + N_BRANCH * d),
                      (w_br_na, l), (w_br_m, l), (w_br_cv, l), (w_out, l)),
        wg=(w_ffn_gate, l), wu=(w_ffn_up, l), wd=(w_ffn_down, l),
    )


def _trunk_layer(x, mod, lw, batch, seq_len, ctx):
    is_ctx = ctx is None
    naqkv, mqv, mkt, ift, cvb, u = _inproj(x, mod, lw["g_pre_mix"], lw["wa"], lw["wkt"], lw["wift"], lw["bift"],
                                           lw["wcv"], seq_len=seq_len, na_dtype=F32 if is_ctx else BF16)
    if is_ctx:
        att = _context_attention(naqkv, batch, seq_len)
        c0 = jnp.zeros((batch, 2 * M_HEADS, M_DK, 2 * M_DV), F32)
        m0 = jnp.zeros((batch, 2 * M_HEADS, LANES), F32)
    else:
        k_ctx, v_ctx, bias, c0, m0 = ctx
        att = _neighborhood_attention(naqkv, k_ctx, v_ctx, bias, batch, seq_len)
    hf, hb, c_new, m_new = _bidir_mlstm(mqv, mkt, ift, c0, m0, batch, seq_len)
    x = _merge(x, mod, att, hf, hb, cvb, u, lw["merge_consts"], seq_len=seq_len)
    x = _ffn(x, mod, lw["g_pre_ffn"], lw["g_post_ffn"], lw["wg"], lw["wu"], lw["wd"], seq_len=seq_len)
    return x, (naqkv, c_new, m_new)


def kernel(x_prompt, x_sample, c, cache_k, cache_v, state_C, state_n, state_m, c_ctx, w_ada, b_ada, g_pre_mix,
           g_post_mix, g_pre_ffn, g_post_ffn, w_in, b_gate, m_b_i, m_b_f, na_rpb, m_norm_g, conv_w, w_br_na,
           w_br_m, w_br_cv, w_out, w_ffn_gate, w_ffn_up, w_ffn_down):
    batch, seq, d = x_prompt.shape
    dec_batch, dec_seq, _ = x_sample.shape
    depth = w_in.shape[0]
    past = cache_k.shape[3]
    assert dec_batch + 1 <= COND_ROWS

    cond = jnp.zeros((COND_ROWS, d), F32).at[:dec_batch].set(c).at[dec_batch].set(c_ctx)
    mod = _ada_modulation(cond, w_ada, b_ada).reshape(depth, COND_ROWS, N_MOD, d)

    xp = x_prompt.reshape(batch * seq, d)
    xs = x_sample.reshape(dec_batch * dec_seq, d)
    ks, vs, cs, ns, ms = [], [], [], [], []
    w_br_na, w_br_m, w_br_cv, w_out, w_ffn_gate, w_ffn_up, w_ffn_down = (
        w.astype(BF16) for w in (w_br_na, w_br_m, w_br_cv, w_out, w_ffn_gate, w_ffn_up, w_ffn_down))
    for l in range(depth):
        lw = _layer_weights(l, w_in, b_gate, m_b_i, m_b_f, g_pre_mix, g_post_mix, g_pre_ffn, g_post_ffn,
                            m_norm_g, conv_w, w_br_na, w_br_m, w_br_cv, w_out, w_ffn_gate, w_ffn_up, w_ffn_down)
        xp, (naqkv, c_l, m_l) = _trunk_layer(xp, mod[l, dec_batch:dec_batch + 1], lw, batch, seq, None)
        heads = lambda a: a.reshape(batch, seq, NA_HEADS, NA_HEAD_DIM).transpose(0, 2, 1, 3)
        ks.append(heads(naqkv[:, NA_WIDTH:2 * NA_WIDTH]))
        vs.append(heads(naqkv[:, 2 * NA_WIDTH:]))
        cs.append(c_l[..., :M_DV].reshape(batch, 2, M_HEADS, M_DK, M_DV))
        ns.append(c_l[..., M_DV].reshape(batch, 2, M_HEADS, M_DK))
        ms.append(m_l[:, :, 0].reshape(batch, 2, M_HEADS))

        tokens = lambda a: a.transpose(0, 2, 1, 3).reshape(dec_batch, past, NA_WIDTH).astype(BF16)
        c0 = jnp.concatenate([state_C[:, l], jnp.broadcast_to(state_n[:, l][..., None], state_C[:, l].shape)],
                             axis=-1).reshape(dec_batch, 2 * M_HEADS, M_DK, 2 * M_DV)
        m0 = jnp.broadcast_to(state_m[:, l].reshape(dec_batch, 2 * M_HEADS, 1), (dec_batch, 2 * M_HEADS, LANES))
        ctx = (tokens(cache_k[:, l]), tokens(cache_v[:, l]), _na_bias_tables(na_rpb[l], dec_seq // GRID_W), c0, m0)
        xs, _ = _trunk_layer(xs, mod[l, :dec_batch], lw, dec_batch, dec_seq, ctx)

    return (xp.reshape(batch, seq, d), xs.reshape(dec_batch, dec_seq, d),
            jnp.stack(ks, axis=1), jnp.stack(vs, axis=1), jnp.stack(cs, axis=1),
            jnp.stack(ns, axis=1), jnp.stack(ms, axis=1))
```

```python
import functools

import numpy as np
import jax
import jax.numpy as jnp
from jax import lax
from jax.experimental import pallas as pl
from jax.experimental.pallas import tpu as pltpu

F32 = jnp.float32
BF16 = jnp.bfloat16

GRID_W = 64
NA_HEADS = 8
NA_HEAD_DIM = 64
NA_WIDTH = NA_HEADS * NA_HEAD_DIM
NA_KH = 8
NA_KW = 16
NA_SCALE = NA_HEAD_DIM ** -0.5
M_HEADS = 4
M_DK = 128
M_DV = 128
M_WIDTH = M_HEADS * M_DV
M_CHUNK = 128
M_SCALE = M_DK ** -0.5
CONV_WIDTH = 512
N_BRANCH = 3
N_MOD = 6
EPS = 1e-6
NEG = -1e30

LANES = 128
SUBLANES = 8
VMEM_LIMIT_BYTES = 56 * 1024 * 1024

TOKEN_TILE = 512
NA_ROWS_PER_BLOCK = 4
M_CHUNKS_PER_STEP = 4
FFN_CHUNKS = 1
ADA_TILE = 1024
COND_ROWS = 16


def _const_spec(shape):
    nd = len(shape)
    return pl.BlockSpec(shape, lambda *_: (0,) * nd, pipeline_mode=pl.Buffered(1))


def _resident(w):
    if isinstance(w, tuple):
        stacked, layer = w
        nd = stacked.ndim - 1
        return stacked, pl.BlockSpec((None,) + stacked.shape[1:], lambda *_: (layer,) + (0,) * nd,
                                     pipeline_mode=pl.Buffered(1))
    return w, _const_spec(w.shape)


def _params(*sem):
    return pltpu.CompilerParams(dimension_semantics=sem, vmem_limit_bytes=VMEM_LIMIT_BYTES)


def _sigmoid(x):
    return 0.5 * jnp.tanh(0.5 * x) + 0.5


def _log_sigmoid(x):
    return jnp.minimum(x, 0.0) - jnp.log1p(jnp.exp(-jnp.abs(x)))


def _rmsnorm(x, g):
    return x * lax.rsqrt(jnp.mean(x * x, axis=-1, keepdims=True) + EPS) * g


def _dot(a, b):
    return jnp.dot(a, b, preferred_element_type=F32)


def _dot_nt(a, b):
    return lax.dot_general(a, b, (((1,), (1,)), ((), ())), preferred_element_type=F32)


def _dot_tn(a, b):
    return lax.dot_general(a, b, (((0,), (0,)), ((), ())), preferred_element_type=F32)


def _ada_kernel(cond_ref, w_ref, b_ref, o_ref):
    c = cond_ref[...]
    s = (c * _sigmoid(c)).astype(BF16)
    o_ref[0] = _dot(s, w_ref[0].astype(BF16)) + b_ref[0]


def _ada_modulation(cond, w_ada, b_ada):
    depth, d, n = w_ada.shape
    return pl.pallas_call(
        _ada_kernel,
        grid=(depth, n // ADA_TILE),
        in_specs=[
            pl.BlockSpec((COND_ROWS, d), lambda l, j: (0, 0)),
            pl.BlockSpec((1, d, ADA_TILE), lambda l, j: (l, 0, j)),
            pl.BlockSpec((1, 1, ADA_TILE), lambda l, j: (l, 0, j)),
        ],
        out_specs=pl.BlockSpec((1, COND_ROWS, ADA_TILE), lambda l, j: (l, 0, j)),
        out_shape=jax.ShapeDtypeStruct((depth, COND_ROWS, n), F32),
        compiler_params=_params("arbitrary", "arbitrary"),
        name="ada_modulation",
    )(cond, w_ada, b_ada.reshape(depth, 1, n))


def _mod_index_map(per_seq, tiles_per_seq):
    if per_seq:
        return lambda i: (i // tiles_per_seq, 0, 0)
    return lambda i: (0, 0, 0)


def _inproj_kernel(x_ref, mod_ref, g_ref, wa_ref, wkt_ref, wift_ref, bift_ref, wcv_ref,
                   naqkv_ref, mqv_ref, mkt_ref, ift_ref, cvb_ref, u_ref):
    mod = mod_ref[0]
    h = _rmsnorm(x_ref[...], g_ref[...]) * (1.0 + mod[1:2]) + mod[0:1]
    hb = h.astype(BF16)
    w = NA_WIDTH
    naqkv_ref[:, 0:w] = (_dot(hb, wa_ref[:, 0:w]) * NA_SCALE).astype(naqkv_ref.dtype)
    naqkv_ref[:, w:2 * w] = _dot(hb, wa_ref[:, w:2 * w]).astype(naqkv_ref.dtype)
    naqkv_ref[:, 2 * w:3 * w] = _dot(hb, wa_ref[:, 2 * w:3 * w]).astype(naqkv_ref.dtype)
    o = 3 * w
    m = M_WIDTH
    mqv_ref[:, 0:m] = (_dot(hb, wa_ref[:, o:o + m]) * M_SCALE).astype(BF16)
    mqv_ref[:, m:2 * m] = _dot(hb, wa_ref[:, o + m:o + 2 * m]).astype(BF16)
    mkt_ref[...] = _dot_nt(wkt_ref[...], hb).astype(BF16)
    ift_ref[...] = _dot_nt(wift_ref[...], hb) + bift_ref[...]
    c = CONV_WIDTH
    cvb_ref[...] = _dot(hb, wcv_ref[:, 0:c])
    u_ref[...] = _dot(hb, wcv_ref[:, c:2 * c]) * _dot(hb, wcv_ref[:, 2 * c:3 * c])


def _inproj(x, mod, g_pre, wa, wkt, wift, bift, wcv, *, seq_len, na_dtype):
    ntok, d = x.shape
    tile = min(TOKEN_TILE, ntok)
    per_seq = mod.shape[0] > 1
    assert ntok % tile == 0 and (not per_seq or seq_len % tile == 0)
    row = lambda n: pl.BlockSpec((tile, n), lambda i: (i, 0))
    col = lambda n: pl.BlockSpec((n, tile), lambda i: (0, i))
    n_gate = 4 * M_HEADS
    return pl.pallas_call(
        _inproj_kernel,
        grid=(ntok // tile,),
        in_specs=[
            row(d),
            pl.BlockSpec((1, N_MOD, d), _mod_index_map(per_seq, max(seq_len // tile, 1))),
            _const_spec(g_pre.shape),
            _const_spec(wa.shape),
            _const_spec(wkt.shape),
            _const_spec(wift.shape),
            _const_spec(bift.shape),
            _const_spec(wcv.shape),
        ],
        out_specs=[row(3 * NA_WIDTH), row(2 * M_WIDTH), col(M_WIDTH), col(n_gate),
                   row(CONV_WIDTH), row(CONV_WIDTH)],
        out_shape=[
            jax.ShapeDtypeStruct((ntok, 3 * NA_WIDTH), na_dtype),
            jax.ShapeDtypeStruct((ntok, 2 * M_WIDTH), BF16),
            jax.ShapeDtypeStruct((M_WIDTH, ntok), BF16),
            jax.ShapeDtypeStruct((n_gate, ntok), F32),
            jax.ShapeDtypeStruct((ntok, CONV_WIDTH), F32),
            jax.ShapeDtypeStruct((ntok, CONV_WIDTH), F32),
        ],
        compiler_params=_params("parallel"),
        name="in_projection",
    )(x, mod, g_pre, wa, wkt, wift, bift, wcv)


def _softmax_pv(scores, values):
    mx = functools.reduce(jnp.maximum, [jnp.max(s, axis=-1, keepdims=True) for s in scores])
    es = [jnp.exp(s - mx) for s in scores]
    den = functools.reduce(jnp.add, [jnp.sum(e, axis=-1, keepdims=True) for e in es])
    acc = functools.reduce(jnp.add, [_dot(e.astype(BF16), v) for e, v in zip(es, values)])
    return acc / den


def _ctx_attn_kernel(q_ref, k_ref, v_ref, o_ref):
    low = lax.broadcasted_iota(jnp.int32, (1, LANES), 1) < NA_HEAD_DIM
    for p in range(NA_WIDTH // LANES):
        sl = slice(p * LANES, (p + 1) * LANES)
        q2 = q_ref[0, :, sl].astype(BF16)
        k2 = k_ref[0, :, sl].astype(BF16)
        v2 = v_ref[0, :, sl].astype(BF16)
        halves = []
        for first in (True, False):
            qm = jnp.where(low == first, q2, jnp.zeros_like(q2))
            halves.append(_softmax_pv([_dot_nt(qm, k2)], [v2]))
        o_ref[0, :, sl] = jnp.where(low, halves[0], halves[1]).astype(o_ref.dtype)


def _context_attention(naqkv, batch, seq):
    a = naqkv.reshape(batch, seq, 3 * NA_WIDTH)
    spec = lambda j: pl.BlockSpec((1, seq, NA_WIDTH), lambda b: (b, 0, j))
    out = pl.pallas_call(
        _ctx_attn_kernel,
        grid=(batch,),
        in_specs=[spec(0), spec(1), spec(2)],
        out_specs=spec(0),
        out_shape=jax.ShapeDtypeStruct((batch, seq, NA_WIDTH), BF16),
        compiler_params=_params("parallel"),
        name="context_attention",
    )(a, a, a)
    return out.reshape(batch * seq, NA_WIDTH)


def _na_geometry(rows):
    r = NA_ROWS_PER_BLOCK
    slab = r + NA_KH - 1
    assert rows % r == 0 and rows >= slab and rows >= NA_KH
    nblk = rows // r
    pats = []
    for i in range(nblk):
        r0 = r * i
        s0 = int(np.clip(r0 - NA_KH // 2, 0, rows - slab))
        q_rows = r0 + np.arange(r)
        k_rows = s0 + np.arange(slab)
        rs = np.clip(q_rows - NA_KH // 2, 0, rows - NA_KH)
        valid = (k_rows[None, :] >= rs[:, None]) & (k_rows[None, :] < rs[:, None] + NA_KH)
        dr = np.clip(k_rows[None, :] - q_rows[:, None] + NA_KH - 1, 0, 2 * NA_KH - 2)
        pats.append((valid, dr))
    same = lambda a, b: np.array_equal(a[0], b[0]) and np.array_equal(a[1][a[0]], b[1][b[0]])
    assert nblk >= 3 and all(same(pats[1], p) for p in pats[1:-1])
    return r, slab, nblk, [pats[0], pats[1], pats[-1]]


def _na_bias_tables(rpb, rows):
    r, slab, _, pats = _na_geometry(rows)
    c = np.arange(GRID_W)
    cs = np.clip(c - NA_KW // 2, 0, GRID_W - NA_KW)
    kc = np.arange(GRID_W)
    col_ok = (kc[None, :] >= cs[:, None]) & (kc[None, :] < cs[:, None] + NA_KW)
    pad = GRID_W - NA_KW
    assert np.all(np.abs(kc[None, :] - c[:, None])[col_ok] <= NA_KW - 1)
    padded = jnp.pad(rpb.astype(F32), ((0, 0), (0, 0), (pad, pad)))
    toeplitz = jnp.stack([padded[:, :, GRID_W - 1 - ci:2 * GRID_W - 1 - ci] for ci in range(GRID_W)], axis=2)
    t4 = jnp.where(col_ok[None, None], toeplitz, NEG)
    masked = jnp.full((rpb.shape[0], GRID_W, GRID_W), NEG, F32)
    out = []
    for valid, dr in pats:
        tile_rows = [jnp.concatenate([t4[:, dr[i, j]] if valid[i, j] else masked for j in range(slab)], axis=-1)
                     for i in range(r)]
        out.append(jnp.concatenate(tile_rows, axis=1))
    return jnp.stack(out)


def _na_kernel(q_ref, k_ref, v_ref, kc_ref, vc_ref, bias_ref, o_ref, *, rows):
    r, slab, _, _ = _na_geometry(rows)
    i = pl.program_id(1)
    s0 = jnp.clip(r * i - NA_KH // 2, 0, rows - slab)
    start = pl.multiple_of(s0 * GRID_W, GRID_W)
    n_slab = slab * GRID_W
    low = lax.broadcasted_iota(jnp.int32, (1, LANES), 1) < NA_HEAD_DIM
    n_pairs = NA_WIDTH // LANES
    lanes = lambda p: slice(p * LANES, (p + 1) * LANES)

    def scores(head):
        p, half = divmod(head, 2)
        q2 = q_ref[0, :, lanes(p)]
        qm = jnp.where(low == (half == 0), q2, jnp.zeros_like(q2))
        s_loc = _dot_nt(qm, k_ref[0, pl.ds(start, n_slab), lanes(p)]) + bias_ref[0, head]
        s_ctx = _dot_nt(qm, kc_ref[0, :, lanes(p)])
        return s_loc, s_ctx

    def weights(s_loc, s_ctx):
        mx = jnp.maximum(jnp.max(s_loc, axis=-1, keepdims=True), jnp.max(s_ctx, axis=-1, keepdims=True))
        return jnp.exp(s_loc - mx).astype(BF16), jnp.exp(s_ctx - mx).astype(BF16)

    def attend(head, e_loc, e_ctx):
        p = head // 2
        ones = lambda n: jnp.ones((n, LANES), BF16)
        acc = (_dot(e_loc, jnp.concatenate([v_ref[0, pl.ds(start, n_slab), lanes(p)], ones(n_slab)], axis=-1))
               + _dot(e_ctx, jnp.concatenate([vc_ref[0, :, lanes(p)], ones(vc_ref.shape[1])], axis=-1)))
        return acc[:, :LANES] / acc[:, LANES:]

    sc, ex, out = {}, {}, {}
    for step in range(NA_HEADS + 2):
        if step < NA_HEADS:
            sc[step] = scores(step)
        if 0 <= step - 1 < NA_HEADS:
            ex[step - 1] = weights(*sc.pop(step - 1))
        if 0 <= step - 2 < NA_HEADS:
            head = step - 2
            out[head] = attend(head, *ex.pop(head))
            if head % 2 == 1:
                o_ref[0, :, lanes(head // 2)] = jnp.where(low, out.pop(head - 1), out.pop(head)).astype(o_ref.dtype)
    assert n_pairs * 2 == NA_HEADS


def _neighborhood_attention(naqkv, k_ctx, v_ctx, bias, batch, n):
    rows = n // GRID_W
    r, slab, nblk, _ = _na_geometry(rows)
    a = naqkv.reshape(batch, n, 3 * NA_WIDTH)
    past = k_ctx.shape[1]
    qb = r * GRID_W
    whole = lambda j: pl.BlockSpec((1, n, NA_WIDTH), lambda b, i: (b, 0, j))
    ctx = pl.BlockSpec((1, past, NA_WIDTH), lambda b, i: (b, 0, 0))
    pattern = lambda b, i: (jnp.where(i == 0, 0, jnp.where(i == nblk - 1, 2, 1)), 0, 0, 0)
    out = pl.pallas_call(
        functools.partial(_na_kernel, rows=rows),
        grid=(batch, nblk),
        in_specs=[
            pl.BlockSpec((1, qb, NA_WIDTH), lambda b, i: (b, i, 0)),
            whole(1), whole(2), ctx, ctx,
            pl.BlockSpec((1, NA_HEADS, qb, slab * GRID_W), pattern),
        ],
        out_specs=pl.BlockSpec((1, qb, NA_WIDTH), lambda b, i: (b, i, 0)),
        out_shape=jax.ShapeDtypeStruct((batch, n, NA_WIDTH), BF16),
        compiler_params=_params("parallel", "arbitrary"),
        name="neighborhood_attention",
    )(a, a, a, k_ctx, v_ctx, bias)
    return out.reshape(batch * n, NA_WIDTH)


def _mlstm_kernel(qf_ref, vf_ref, ktf_ref, iff_ref, qb_ref, vb_ref, ktb_ref, ifb_ref, c0_ref, m0_ref,
                  hf_ref, hb_ref, c_out_ref, m_out_ref, c_scr, m_scr):
    step = pl.program_id(1)
    length = M_CHUNK
    nh = M_HEADS

    @pl.when(step == 0)
    def _():
        c_scr[...] = c0_ref[0]
        m_scr[...] = m0_ref[0]

    t_idx = lax.broadcasted_iota(jnp.int32, (length, length), 0)
    s_idx = lax.broadcasted_iota(jnp.int32, (length, length), 1)
    ones = jnp.ones((length, M_DV), BF16)

    n_sub = qf_ref.shape[0] // length
    chains = []
    m_carry = [m_scr[j:j + 1, 0:1] for j in range(2 * nh)]
    for k in range(n_sub):
        for direction in range(2):
            q_ref, v_ref, kt_ref, if_ref = ((qf_ref, vf_ref, ktf_ref, iff_ref) if direction == 0
                                            else (qb_ref, vb_ref, ktb_ref, ifb_ref))
            sub = k if direction == 0 else n_sub - 1 - k
            tok = slice(sub * length, (sub + 1) * length)
            earlier = (s_idx <= t_idx) if direction == 0 else (s_idx >= t_idx)
            pre = if_ref[:, tok]
            log_f = _log_sigmoid(pre[2 * nh:4 * nh, :])
            tri = jnp.where((t_idx <= s_idx) if direction == 0 else (t_idx >= s_idx), 1.0, 0.0)
            b_rows = jnp.dot(log_f, tri, precision=lax.Precision.HIGHEST, preferred_element_type=F32)
            last = length - 1 if direction == 0 else 0
            for head in range(nh):
                j = direction * nh + head
                b_row = b_rows[j:j + 1, :]
                a_row = pre[j:j + 1, :] - b_row
                b_tot = b_row[:, last:last + 1]
                m_state = m_carry[j]
                g_row = b_tot + a_row
                m_new = jnp.maximum(b_tot + m_state, jnp.max(g_row, axis=-1, keepdims=True))
                m_carry[j] = m_new
                chains.append(dict(
                    k=k, j=j, sl=slice(head * M_DK, (head + 1) * M_DK), tok=tok,
                    q_ref=q_ref, v_ref=v_ref, kt_ref=kt_ref, earlier=earlier,
                    h_ref=hf_ref if direction == 0 else hb_ref,
                    f_row=log_f[j:j + 1, :], a_row=a_row, m_state=m_state,
                    decay=jnp.exp(b_tot + m_state - m_new), k_weight=jnp.exp(g_row - m_new)))

    for ch in chains:
        a_mask = jnp.where(ch["earlier"], ch["a_row"], NEG)
        m_col = jnp.maximum(ch["m_state"], jnp.max(a_mask, axis=-1, keepdims=True))
        b_col = jnp.sum(jnp.where(ch["earlier"], ch["f_row"], 0.0), axis=-1, keepdims=True)
        ch["a_mask"] = a_mask
        ch["m_rep"] = jnp.broadcast_to(m_col, (length, M_DV))
        ch["mt_rep"] = jnp.broadcast_to(b_col + m_col, (length, M_DV))
    for ch in chains:
        q = ch["q_ref"][ch["tok"], ch["sl"]]
        s = _dot(q, ch["kt_ref"][ch["sl"], ch["tok"]]) * jnp.exp(ch.pop("a_mask") - ch["m_rep"])
        ch["q"] = q
        ch["s"] = s.astype(BF16)
    for k in range(n_sub):
        group = [ch for ch in chains if ch["k"] == k]
        for ch in group:
            v_aug = jnp.concatenate([ch["v_ref"][ch["tok"], ch["sl"]], ones], axis=-1)
            w_int = jnp.exp(ch["m_state"] - ch["m_rep"])
            inter = _dot(ch.pop("q"), c_scr[ch["j"]].astype(BF16))
            intra = _dot(ch.pop("s"), v_aug)
            num = w_int * inter[:, :M_DV] + intra[:, :M_DV]
            den = w_int * inter[:, M_DV:] + intra[:, M_DV:]
            ch["h_ref"][ch["tok"], ch["sl"]] = num / jnp.maximum(jnp.abs(den), jnp.exp(-ch["mt_rep"]))
            ch["v_aug"] = v_aug
        for ch in group:
            j = ch["j"]
            kw = (ch["kt_ref"][ch["sl"], ch["tok"]].astype(F32) * ch["k_weight"]).astype(BF16)
            c_scr[j] = ch["decay"] * c_scr[j] + _dot(kw, ch.pop("v_aug"))
    for j in range(2 * nh):
        m_scr[j:j + 1, :] = jnp.broadcast_to(m_carry[j], (1, LANES))

    @pl.when(step == pl.num_programs(1) - 1)
    def _():
        c_out_ref[0] = c_scr[...]
        m_out_ref[0] = m_scr[...]


def _bidir_mlstm(mqv, mkt, ift, c0, m0, batch, n):
    span = min(M_CHUNKS_PER_STEP * M_CHUNK, n)
    assert n % span == 0 and span % M_CHUNK == 0
    nc = n // span
    nh2 = 2 * M_HEADS
    fwd = lambda b, c: b * nc + c
    bwd = lambda b, c: b * nc + nc - 1 - c
    tok = lambda j, at: pl.BlockSpec((span, M_WIDTH), lambda b, c: (at(b, c), j))
    feat = lambda rows, at: pl.BlockSpec((rows, span), lambda b, c: (0, at(b, c)))
    c_spec = pl.BlockSpec((1, nh2, M_DK, 2 * M_DV), lambda b, c: (b, 0, 0, 0))
    m_spec = pl.BlockSpec((1, nh2, LANES), lambda b, c: (b, 0, 0))
    stream = lambda at: [tok(0, at), tok(1, at), feat(M_WIDTH, at), feat(2 * nh2, at)]
    return pl.pallas_call(
        _mlstm_kernel,
        grid=(batch, nc),
        in_specs=stream(fwd) + stream(bwd) + [c_spec, m_spec],
        out_specs=[tok(0, fwd), tok(0, bwd), c_spec, m_spec],
        out_shape=[
            jax.ShapeDtypeStruct((batch * n, M_WIDTH), F32),
            jax.ShapeDtypeStruct((batch * n, M_WIDTH), F32),
            jax.ShapeDtypeStruct(c0.shape, F32),
            jax.ShapeDtypeStruct(m0.shape, F32),
        ],
        scratch_shapes=[
            pltpu.VMEM((nh2, M_DK, 2 * M_DV), F32),
            pltpu.VMEM((nh2, LANES), F32),
        ],
        compiler_params=_params("parallel", "arbitrary"),
        name="bidir_mlstm",
    )(mqv, mqv, mkt, ift, mqv, mqv, mkt, ift, c0, m0)


def _merge_kernel(x_ref, mod_ref, att_ref, hf_ref, hb_ref, cvb_ref, u_ref, uprev_ref, unext_ref,
                  gpre_ref, gpost_ref, mng_ref, convw_ref, bgate_ref,
                  wmo_ref, wgate_ref, wna_ref, wm_ref, wcv_ref, wout_ref,
                  o_ref, *, seq_len):
    tile, d = x_ref.shape
    x = x_ref[...]
    mod = mod_ref[0]
    h = _rmsnorm(x, gpre_ref[...]) * (1.0 + mod[1:2]) + mod[0:1]
    hb16 = h.astype(BF16)

    m_o = _dot(hb16, wmo_ref[...])
    hm_parts = []
    for head in range(M_HEADS):
        sl = slice(head * M_DV, (head + 1) * M_DV)
        hh = hf_ref[:, sl] + hb_ref[:, sl]
        hm_parts.append(hh * lax.rsqrt(jnp.mean(hh * hh, axis=-1, keepdims=True) + EPS))
    hm = jnp.concatenate(hm_parts, axis=-1) * mng_ref[...] * _sigmoid(m_o)

    u = u_ref[...]
    row = lax.broadcasted_iota(jnp.int32, (tile, 1), 0)
    pos = (pl.program_id(0) * tile + row) % seq_len
    u_before = jnp.where(row == 0, uprev_ref[SUBLANES - 1:SUBLANES, :], pltpu.roll(u, 1, 0))
    u_after = jnp.where(row == tile - 1, unext_ref[0:1, :], pltpu.roll(u, tile - 1, 0))
    u_before = jnp.where(pos == 0, 0.0, u_before)
    u_after = jnp.where(pos == seq_len - 1, 0.0, u_after)
    cw = convw_ref[...]
    hc = cvb_ref[...] * (u_before * cw[0:1] + u * cw[1:2] + u_after * cw[2:3])

    bg = bgate_ref[...]
    merged = (_sigmoid(_dot(hb16, wgate_ref[:, 0:d]) + bg[:, 0:d]) * _dot(att_ref[...], wna_ref[...])
              + _sigmoid(_dot(hb16, wgate_ref[:, d:2 * d]) + bg[:, d:2 * d]) * _dot(hm.astype(BF16), wm_ref[...])
              + _sigmoid(_dot(hb16, wgate_ref[:, 2 * d:3 * d]) + bg[:, 2 * d:3 * d])
              * _dot(hc.astype(BF16), wcv_ref[...]))
    y = _dot(merged.astype(BF16), wout_ref[...])
    o_ref[...] = x + mod[2:3] * _rmsnorm(y, gpost_ref[...])


def _merge(x, mod, att, hf, hb, cvb, u, consts, *, seq_len):
    ntok, d = x.shape
    tile = min(TOKEN_TILE, ntok)
    per_seq = mod.shape[0] > 1
    assert ntok % tile == 0 and (not per_seq or seq_len % tile == 0)
    assert tile % seq_len == 0 or seq_len % tile == 0
    row = lambda n: pl.BlockSpec((tile, n), lambda i: (i, 0))
    halo = tile // SUBLANES
    last_halo = ntok // SUBLANES - 1
    const_ops, const_specs = zip(*[_resident(c) for c in consts])
    return pl.pallas_call(
        functools.partial(_merge_kernel, seq_len=seq_len),
        grid=(ntok // tile,),
        in_specs=[
            row(d),
            pl.BlockSpec((1, N_MOD, d), _mod_index_map(per_seq, max(seq_len // tile, 1))),
            row(NA_WIDTH), row(M_WIDTH), row(M_WIDTH), row(CONV_WIDTH), row(CONV_WIDTH),
            pl.BlockSpec((SUBLANES, CONV_WIDTH), lambda i: (jnp.maximum(i * halo - 1, 0), 0)),
            pl.BlockSpec((SUBLANES, CONV_WIDTH), lambda i: (jnp.minimum((i + 1) * halo, last_halo), 0)),
        ] + list(const_specs),
        out_specs=row(d),
        out_shape=jax.ShapeDtypeStruct((ntok, d), F32),
        compiler_params=_params("parallel"),
        name="branch_merge",
    )(x, mod, att, hf, hb, cvb, u, u, u, *const_ops)


def _ffn_kernel(x_ref, mod_ref, gpre_ref, gpost_ref, wg_ref, wu_ref, wd_ref, o_ref):
    x = x_ref[...]
    mod = mod_ref[0]
    h = _rmsnorm(x, gpre_ref[...]) * (1.0 + mod[4:5]) + mod[3:4]
    hb16 = h.astype(BF16)
    d_ff = wg_ref.shape[1]
    chunk = d_ff // FFN_CHUNKS
    ff = None
    for c in range(FFN_CHUNKS):
        sl = slice(c * chunk, (c + 1) * chunk)
        gate = _dot(hb16, wg_ref[:, sl])
        act = (gate * _sigmoid(gate) * _dot(hb16, wu_ref[:, sl])).astype(BF16)
        part = _dot(act, wd_ref[sl, :])
        ff = part if ff is None else ff + part
    o_ref[...] = x + mod[5:6] * _rmsnorm(ff, gpost_ref[...])


def _ffn(x, mod, g_pre, g_post, wg, wu, wd, *, seq_len):
    ntok, d = x.shape
    tile = min(TOKEN_TILE, ntok)
    per_seq = mod.shape[0] > 1
    assert ntok % tile == 0 and (not per_seq or seq_len % tile == 0)
    const_ops, const_specs = zip(*[_resident(c) for c in (g_pre, g_post, wg, wu, wd)])
    assert const_ops[2].shape[-1] % (FFN_CHUNKS * 2 * LANES) == 0
    row = pl.BlockSpec((tile, d), lambda i: (i, 0))
    return pl.pallas_call(
        _ffn_kernel,
        grid=(ntok // tile,),
        in_specs=[row, pl.BlockSpec((1, N_MOD, d), _mod_index_map(per_seq, max(seq_len // tile, 1)))]
        + list(const_specs),
        out_specs=row,
        out_shape=jax.ShapeDtypeStruct((ntok, d), F32),
        compiler_params=_params("parallel"),
        name="swiglu_ffn",
    )(x, mod, *const_ops)


def _layer_weights(l, w_in, b_gate, m_b_i, m_b_f, g_pre_mix, g_post_mix, g_pre_ffn, g_post_ffn, m_norm_g, conv_w,
                   w_br_na, w_br_m, w_br_cv, w_out, w_ffn_gate, w_ffn_up, w_ffn_down):
    d = w_in.shape[1]
    o_mo = 3 * NA_WIDTH + 3 * M_WIDTH
    o_if = o_mo + M_WIDTH
    o_cv = o_if + 4 * M_HEADS
    o_gate = o_cv + 3 * CONV_WIDTH
    assert w_in.shape[2] == o_gate + N_BRANCH * d
    o_mk = 3 * NA_WIDTH + M_WIDTH
    o_mv = o_mk + M_WIDTH
    cols = lambda a, b: w_in[l, :, a:b].astype(BF16)
    bift = jnp.concatenate([m_b_i[l].reshape(-1), m_b_f[l].reshape(-1)]).reshape(4 * M_HEADS, 1)
    vec = lambda a: a[l].reshape(1, -1)
    return dict(
        wa=jnp.concatenate([cols(0, o_mk), cols(o_mv, o_mo)], axis=1),
        wkt=cols(o_mk, o_mv).T, wift=cols(o_if, o_cv).T, bift=bift, wcv=cols(o_cv, o_gate),
        g_pre_mix=vec(g_pre_mix), g_pre_ffn=vec(g_pre_ffn), g_post_ffn=vec(g_post_ffn),
        merge_consts=(vec(g_pre_mix), vec(g_post_mix), vec(m_norm_g), conv_w[l], vec(b_gate),
                      cols(o_mo, o_if), cols(o_gate, o_gate + N_BRANCH * d),
                      (w_br_na, l), (w_br_m, l), (w_br_cv, l), (w_out, l)),
        wg=(w_ffn_gate, l), wu=(w_ffn_up, l), wd=(w_ffn_down, l),
    )


def _trunk_layer(x, mod, lw, batch, seq_len, ctx):
    is_ctx = ctx is None
    naqkv, mqv, mkt, ift, cvb, u = _inproj(x, mod, lw["g_pre_mix"], lw["wa"], lw["wkt"], lw["wift"], lw["bift"],
                                           lw["wcv"], seq_len=seq_len, na_dtype=F32 if is_ctx else BF16)
    if is_ctx:
        att = _context_attention(naqkv, batch, seq_len)
        c0 = jnp.zeros((batch, 2 * M_HEADS, M_DK, 2 * M_DV), F32)
        m0 = jnp.zeros((batch, 2 * M_HEADS, LANES), F32)
    else:
        k_ctx, v_ctx, bias, c0, m0 = ctx
        att = _neighborhood_attention(naqkv, k_ctx, v_ctx, bias, batch, seq_len)
    hf, hb, c_new, m_new = _bidir_mlstm(mqv, mkt, ift, c0, m0, batch, seq_len)
    x = _merge(x, mod, att, hf, hb, cvb, u, lw["merge_consts"], seq_len=seq_len)
    x = _ffn(x, mod, lw["g_pre_ffn"], lw["g_post_ffn"], lw["wg"], lw["wu"], lw["wd"], seq_len=seq_len)
    return x, (naqkv, c_new, m_new)


def kernel(x_prompt, x_sample, c, cache_k, cache_v, state_C, state_n, state_m, c_ctx, w_ada, b_ada, g_pre_mix,
           g_post_mix, g_pre_ffn, g_post_ffn, w_in, b_gate, m_b_i, m_b_f, na_rpb, m_norm_g, conv_w, w_br_na,
           w_br_m, w_br_cv, w_out, w_ffn_gate, w_ffn_up, w_ffn_down):
    batch, seq, d = x_prompt.shape
    dec_batch, dec_seq, _ = x_sample.shape
    depth = w_in.shape[0]
    past = cache_k.shape[3]
    assert dec_batch + 1 <= COND_ROWS

    cond = jnp.zeros((COND_ROWS, d), F32).at[:dec_batch].set(c).at[dec_batch].set(c_ctx)
    mod = _ada_modulation(cond, w_ada, b_ada).reshape(depth, COND_ROWS, N_MOD, d)

    xp = x_prompt.reshape(batch * seq, d)
    xs = x_sample.reshape(dec_batch * dec_seq, d)
    ks, vs, cs, ns, ms = [], [], [], [], []
    w_br_na, w_br_m, w_br_cv, w_out, w_ffn_gate, w_ffn_up, w_ffn_down = (
        w.astype(BF16) for w in (w_br_na, w_br_m, w_br_cv, w_out, w_ffn_gate, w_ffn_up, w_ffn_down))
    for l in range(depth):
        lw = _layer_weights(l, w_in, b_gate, m_b_i, m_b_f, g_pre_mix, g_post_mix, g_pre_ffn, g_post_ffn,
                            m_norm_g, conv_w, w_br_na, w_br_m, w_br_cv, w_out, w_ffn_gate, w_ffn_up, w_ffn_down)
        xp, (naqkv, c_l, m_l) = _trunk_layer(xp, mod[l, dec_batch:dec_batch + 1], lw, batch, seq, None)
        heads = lambda a: a.reshape(batch, seq, NA_HEADS, NA_HEAD_DIM).transpose(0, 2, 1, 3)
        ks.append(heads(naqkv[:, NA_WIDTH:2 * NA_WIDTH]))
        vs.append(heads(naqkv[:, 2 * NA_WIDTH:]))
        cs.append(c_l[..., :M_DV].reshape(batch, 2, M_HEADS, M_DK, M_DV))
        ns.append(c_l[..., M_DV].reshape(batch, 2, M_HEADS, M_DK))
        ms.append(m_l[:, :, 0].reshape(batch, 2, M_HEADS))

        tokens = lambda a: a.transpose(0, 2, 1, 3).reshape(dec_batch, past, NA_WIDTH).astype(BF16)
        c0 = jnp.concatenate([state_C[:, l], jnp.broadcast_to(state_n[:, l][..., None], state_C[:, l].shape)],
                             axis=-1).reshape(dec_batch, 2 * M_HEADS, M_DK, 2 * M_DV)
        m0 = jnp.broadcast_to(state_m[:, l].reshape(dec_batch, 2 * M_HEADS, 1), (dec_batch, 2 * M_HEADS, LANES))
        ctx = (tokens(cache_k[:, l]), tokens(cache_v[:, l]), _na_bias_tables(na_rpb[l], dec_seq // GRID_W), c0, m0)
        xs, _ = _trunk_layer(xs, mod[l, :dec_batch], lw, dec_batch, dec_seq, ctx)

    return (xp.reshape(batch, seq, d), xs.reshape(dec_batch, dec_seq, d),
            jnp.stack(ks, axis=1), jnp.stack(vs, axis=1), jnp.stack(cs, axis=1),
            jnp.stack(ns, axis=1), jnp.stack(ms, axis=1))
```

```python
import functools

import numpy as np
import jax
import jax.numpy as jnp
from jax import lax
from jax.experimental import pallas as pl
from jax.experimental.pallas import tpu as pltpu

F32 = jnp.float32
BF16 = jnp.bfloat16

GRID_W = 64
NA_HEADS = 8
NA_HEAD_DIM = 64
NA_WIDTH = NA_HEADS * NA_HEAD_DIM
NA_KH = 8
NA_KW = 16
NA_SCALE = NA_HEAD_DIM ** -0.5
M_HEADS = 4
M_DK = 128
M_DV = 128
M_WIDTH = M_HEADS * M_DV
M_CHUNK = 128
M_SCALE = M_DK ** -0.5
CONV_WIDTH = 512
N_BRANCH = 3
N_MOD = 6
EPS = 1e-6
NEG = -1e30

LANES = 128
SUBLANES = 8
VMEM_LIMIT_BYTES = 56 * 1024 * 1024

TOKEN_TILE = 512
NA_ROWS_PER_BLOCK = 4
M_CHUNKS_PER_STEP = 8
FFN_ROW_GROUPS = 2
FFN_CHUNKS = 1
ADA_TILE = 1024
COND_ROWS = 16


def _const_spec(shape):
    nd = len(shape)
    return pl.BlockSpec(shape, lambda *_: (0,) * nd, pipeline_mode=pl.Buffered(1))


def _resident(w):
    if isinstance(w, tuple):
        stacked, layer = w
        nd = stacked.ndim - 1
        return stacked, pl.BlockSpec((None,) + stacked.shape[1:], lambda *_: (layer,) + (0,) * nd,
                                     pipeline_mode=pl.Buffered(1))
    return w, _const_spec(w.shape)


def _params(*sem):
    return pltpu.CompilerParams(dimension_semantics=sem, vmem_limit_bytes=VMEM_LIMIT_BYTES)


def _sigmoid(x):
    return 0.5 * jnp.tanh(0.5 * x) + 0.5


def _log_sigmoid(x):
    return jnp.minimum(x, 0.0) - jnp.log1p(jnp.exp(-jnp.abs(x)))


def _rmsnorm(x, g):
    return x * lax.rsqrt(jnp.mean(x * x, axis=-1, keepdims=True) + EPS) * g


def _dot(a, b):
    return jnp.dot(a, b, preferred_element_type=F32)


def _dot_nt(a, b):
    return lax.dot_general(a, b, (((1,), (1,)), ((), ())), preferred_element_type=F32)


def _ada_kernel(cond_ref, w_ref, b_ref, o_ref):
    c = cond_ref[...]
    s = (c * _sigmoid(c)).astype(BF16)
    o_ref[0] = _dot(s, w_ref[0].astype(BF16)) + b_ref[0]


def _ada_modulation(cond, w_ada, b_ada):
    depth, d, n = w_ada.shape
    return pl.pallas_call(
        _ada_kernel,
        grid=(depth, n // ADA_TILE),
        in_specs=[
            pl.BlockSpec((COND_ROWS, d), lambda l, j: (0, 0)),
            pl.BlockSpec((1, d, ADA_TILE), lambda l, j: (l, 0, j)),
            pl.BlockSpec((1, 1, ADA_TILE), lambda l, j: (l, 0, j)),
        ],
        out_specs=pl.BlockSpec((1, COND_ROWS, ADA_TILE), lambda l, j: (l, 0, j)),
        out_shape=jax.ShapeDtypeStruct((depth, COND_ROWS, n), F32),
        compiler_params=_params("arbitrary", "arbitrary"),
        name="ada_modulation",
    )(cond, w_ada, b_ada.reshape(depth, 1, n))


def _mod_index_map(per_seq, tiles_per_seq):
    if per_seq:
        return lambda i: (i // tiles_per_seq, 0, 0)
    return lambda i: (0, 0, 0)


def _inproj_kernel(x_ref, mod_ref, g_ref, wa_ref, wkt_ref, wift_ref, bift_ref, wcv_ref,
                   naqkv_ref, mqv_ref, mkt_ref, ift_ref, cvb_ref, u_ref):
    mod = mod_ref[0]
    h = _rmsnorm(x_ref[...], g_ref[...]) * (1.0 + mod[1:2]) + mod[0:1]
    hb = h.astype(BF16)
    w = NA_WIDTH
    naqkv_ref[:, 0:w] = (_dot(hb, wa_ref[:, 0:w]) * NA_SCALE).astype(naqkv_ref.dtype)
    naqkv_ref[:, w:2 * w] = _dot(hb, wa_ref[:, w:2 * w]).astype(naqkv_ref.dtype)
    naqkv_ref[:, 2 * w:3 * w] = _dot(hb, wa_ref[:, 2 * w:3 * w]).astype(naqkv_ref.dtype)
    o = 3 * w
    m = M_WIDTH
    mqv_ref[:, 0:m] = (_dot(hb, wa_ref[:, o:o + m]) * M_SCALE).astype(BF16)
    mqv_ref[:, m:2 * m] = _dot(hb, wa_ref[:, o + m:o + 2 * m]).astype(BF16)
    mkt_ref[...] = _dot_nt(wkt_ref[...], hb).astype(BF16)
    ift_ref[...] = _dot_nt(wift_ref[...], hb) + bift_ref[...]
    c = CONV_WIDTH
    cvb_ref[...] = _dot(hb, wcv_ref[:, 0:c])
    u_ref[...] = _dot(hb, wcv_ref[:, c:2 * c]) * _dot(hb, wcv_ref[:, 2 * c:3 * c])


def _inproj(x, mod, g_pre, wa, wkt, wift, bift, wcv, *, seq_len, na_dtype):
    ntok, d = x.shape
    tile = min(TOKEN_TILE, ntok)
    per_seq = mod.shape[0] > 1
    assert ntok % tile == 0 and (not per_seq or seq_len % tile == 0)
    row = lambda n: pl.BlockSpec((tile, n), lambda i: (i, 0))
    col = lambda n: pl.BlockSpec((n, tile), lambda i: (0, i))
    n_gate = 4 * M_HEADS
    return pl.pallas_call(
        _inproj_kernel,
        grid=(ntok // tile,),
        in_specs=[
            row(d),
            pl.BlockSpec((1, N_MOD, d), _mod_index_map(per_seq, max(seq_len // tile, 1))),
            _const_spec(g_pre.shape),
            _const_spec(wa.shape),
            _const_spec(wkt.shape),
            _const_spec(wift.shape),
            _const_spec(bift.shape),
            _const_spec(wcv.shape),
        ],
        out_specs=[row(3 * NA_WIDTH), row(2 * M_WIDTH), col(M_WIDTH), col(n_gate),
                   row(CONV_WIDTH), row(CONV_WIDTH)],
        out_shape=[
            jax.ShapeDtypeStruct((ntok, 3 * NA_WIDTH), na_dtype),
            jax.ShapeDtypeStruct((ntok, 2 * M_WIDTH), BF16),
            jax.ShapeDtypeStruct((M_WIDTH, ntok), BF16),
            jax.ShapeDtypeStruct((n_gate, ntok), F32),
            jax.ShapeDtypeStruct((ntok, CONV_WIDTH), F32),
            jax.ShapeDtypeStruct((ntok, CONV_WIDTH), F32),
        ],
        compiler_params=_params("parallel"),
        name="in_projection",
    )(x, mod, g_pre, wa, wkt, wift, bift, wcv)


def _softmax_pv(scores, values):
    mx = functools.reduce(jnp.maximum, [jnp.max(s, axis=-1, keepdims=True) for s in scores])
    es = [jnp.exp(s - mx) for s in scores]
    den = functools.reduce(jnp.add, [jnp.sum(e, axis=-1, keepdims=True) for e in es])
    acc = functools.reduce(jnp.add, [_dot(e.astype(BF16), v) for e, v in zip(es, values)])
    return acc / den


def _ctx_attn_kernel(q_ref, k_ref, v_ref, o_ref):
    low = lax.broadcasted_iota(jnp.int32, (1, LANES), 1) < NA_HEAD_DIM
    for p in range(NA_WIDTH // LANES):
        sl = slice(p * LANES, (p + 1) * LANES)
        q2 = q_ref[0, :, sl].astype(BF16)
        k2 = k_ref[0, :, sl].astype(BF16)
        v2 = v_ref[0, :, sl].astype(BF16)
        halves = []
        for first in (True, False):
            qm = jnp.where(low == first, q2, jnp.zeros_like(q2))
            halves.append(_softmax_pv([_dot_nt(qm, k2)], [v2]))
        o_ref[0, :, sl] = jnp.where(low, halves[0], halves[1]).astype(o_ref.dtype)


def _context_attention(naqkv, batch, seq):
    a = naqkv.reshape(batch, seq, 3 * NA_WIDTH)
    spec = lambda j: pl.BlockSpec((1, seq, NA_WIDTH), lambda b: (b, 0, j))
    out = pl.pallas_call(
        _ctx_attn_kernel,
        grid=(batch,),
        in_specs=[spec(0), spec(1), spec(2)],
        out_specs=spec(0),
        out_shape=jax.ShapeDtypeStruct((batch, seq, NA_WIDTH), BF16),
        compiler_params=_params("parallel"),
        name="context_attention",
    )(a, a, a)
    return out.reshape(batch * seq, NA_WIDTH)


def _na_geometry(rows):
    r = NA_ROWS_PER_BLOCK
    slab = r + NA_KH - 1
    assert rows % r == 0 and rows >= slab and rows >= NA_KH
    nblk = rows // r
    pats = []
    for i in range(nblk):
        r0 = r * i
        s0 = int(np.clip(r0 - NA_KH // 2, 0, rows - slab))
        q_rows = r0 + np.arange(r)
        k_rows = s0 + np.arange(slab)
        rs = np.clip(q_rows - NA_KH // 2, 0, rows - NA_KH)
        valid = (k_rows[None, :] >= rs[:, None]) & (k_rows[None, :] < rs[:, None] + NA_KH)
        dr = np.clip(k_rows[None, :] - q_rows[:, None] + NA_KH - 1, 0, 2 * NA_KH - 2)
        pats.append((valid, dr))
    same = lambda a, b: np.array_equal(a[0], b[0]) and np.array_equal(a[1][a[0]], b[1][b[0]])
    assert nblk >= 3 and all(same(pats[1], p) for p in pats[1:-1])
    return r, slab, nblk, [pats[0], pats[1], pats[-1]]


def _na_bias_tables(rpb, rows):
    r, slab, _, pats = _na_geometry(rows)
    c = np.arange(GRID_W)
    cs = np.clip(c - NA_KW // 2, 0, GRID_W - NA_KW)
    kc = np.arange(GRID_W)
    col_ok = (kc[None, :] >= cs[:, None]) & (kc[None, :] < cs[:, None] + NA_KW)
    pad = GRID_W - NA_KW
    assert np.all(np.abs(kc[None, :] - c[:, None])[col_ok] <= NA_KW - 1)
    padded = jnp.pad(rpb.astype(F32), ((0, 0), (0, 0), (pad, pad)))
    toeplitz = jnp.stack([padded[:, :, GRID_W - 1 - ci:2 * GRID_W - 1 - ci] for ci in range(GRID_W)], axis=2)
    t4 = jnp.where(col_ok[None, None], toeplitz, NEG)
    masked = jnp.full((rpb.shape[0], GRID_W, GRID_W), NEG, F32)
    out = []
    for valid, dr in pats:
        tile_rows = [jnp.concatenate([t4[:, dr[i, j]] if valid[i, j] else masked for j in range(slab)], axis=-1)
                     for i in range(r)]
        out.append(jnp.concatenate(tile_rows, axis=1))
    return jnp.stack(out)


def _na_kernel(q_ref, k_ref, v_ref, kc_ref, vc_ref, bias_ref, o_ref, *, rows):
    r, slab, _, _ = _na_geometry(rows)
    i = pl.program_id(1)
    s0 = jnp.clip(r * i - NA_KH // 2, 0, rows - slab)
    start = pl.multiple_of(s0 * GRID_W, GRID_W)
    n_slab = slab * GRID_W
    low = lax.broadcasted_iota(jnp.int32, (1, LANES), 1) < NA_HEAD_DIM
    n_pairs = NA_WIDTH // LANES
    lanes = lambda p: slice(p * LANES, (p + 1) * LANES)

    def scores(head):
        p, half = divmod(head, 2)
        q2 = q_ref[0, :, lanes(p)]
        qm = jnp.where(low == (half == 0), q2, jnp.zeros_like(q2))
        s_loc = _dot_nt(qm, k_ref[0, pl.ds(start, n_slab), lanes(p)]) + bias_ref[0, head]
        s_ctx = _dot_nt(qm, kc_ref[0, :, lanes(p)])
        return s_loc, s_ctx

    def weights(s_loc, s_ctx):
        mx = jnp.maximum(jnp.max(s_loc, axis=-1, keepdims=True), jnp.max(s_ctx, axis=-1, keepdims=True))
        return jnp.exp(s_loc - mx).astype(BF16), jnp.exp(s_ctx - mx).astype(BF16)

    def attend(head, e_loc, e_ctx):
        p = head // 2
        ones = lambda n: jnp.ones((n, LANES), BF16)
        acc = (_dot(e_loc, jnp.concatenate([v_ref[0, pl.ds(start, n_slab), lanes(p)], ones(n_slab)], axis=-1))
               + _dot(e_ctx, jnp.concatenate([vc_ref[0, :, lanes(p)], ones(vc_ref.shape[1])], axis=-1)))
        return acc[:, :LANES] / acc[:, LANES:]

    sc, ex, out = {}, {}, {}
    for step in range(NA_HEADS + 2):
        if step < NA_HEADS:
            sc[step] = scores(step)
        if 0 <= step - 1 < NA_HEADS:
            ex[step - 1] = weights(*sc.pop(step - 1))
        if 0 <= step - 2 < NA_HEADS:
            head = step - 2
            out[head] = attend(head, *ex.pop(head))
            if head % 2 == 1:
                o_ref[0, :, lanes(head // 2)] = jnp.where(low, out.pop(head - 1), out.pop(head)).astype(o_ref.dtype)
    assert n_pairs * 2 == NA_HEADS


def _neighborhood_attention(naqkv, k_ctx, v_ctx, bias, batch, n):
    rows = n // GRID_W
    r, slab, nblk, _ = _na_geometry(rows)
    a = naqkv.reshape(batch, n, 3 * NA_WIDTH)
    past = k_ctx.shape[1]
    qb = r * GRID_W
    whole = lambda j: pl.BlockSpec((1, n, NA_WIDTH), lambda b, i: (b, 0, j))
    ctx = pl.BlockSpec((1, past, NA_WIDTH), lambda b, i: (b, 0, 0))
    pattern = lambda b, i: (jnp.where(i == 0, 0, jnp.where(i == nblk - 1, 2, 1)), 0, 0, 0)
    out = pl.pallas_call(
        functools.partial(_na_kernel, rows=rows),
        grid=(batch, nblk),
        in_specs=[
            pl.BlockSpec((1, qb, NA_WIDTH), lambda b, i: (b, i, 0)),
            whole(1), whole(2), ctx, ctx,
            pl.BlockSpec((1, NA_HEADS, qb, slab * GRID_W), pattern),
        ],
        out_specs=pl.BlockSpec((1, qb, NA_WIDTH), lambda b, i: (b, i, 0)),
        out_shape=jax.ShapeDtypeStruct((batch, n, NA_WIDTH), BF16),
        compiler_params=_params("parallel", "arbitrary"),
        name="neighborhood_attention",
    )(a, a, a, k_ctx, v_ctx, bias)
    return out.reshape(batch * n, NA_WIDTH)


def _mlstm_kernel(qf_ref, vf_ref, ktf_ref, iff_ref, qb_ref, vb_ref, ktb_ref, ifb_ref, c0_ref, m0_ref,
                  hf_ref, hb_ref, c_out_ref, m_out_ref, c_scr, m_scr):
    step = pl.program_id(1)
    length = M_CHUNK
    nh = M_HEADS

    @pl.when(step == 0)
    def _():
        c_scr[...] = c0_ref[0]
        m_scr[...] = m0_ref[0]

    t_idx = lax.broadcasted_iota(jnp.int32, (length, length), 0)
    s_idx = lax.broadcasted_iota(jnp.int32, (length, length), 1)
    ones = jnp.ones((length, M_DV), BF16)

    n_sub = qf_ref.shape[0] // length
    chains = []
    m_carry = [m_scr[j:j + 1, 0:1] for j in range(2 * nh)]
    for k in range(n_sub):
        for direction in range(2):
            q_ref, v_ref, kt_ref, if_ref = ((qf_ref, vf_ref, ktf_ref, iff_ref) if direction == 0
                                            else (qb_ref, vb_ref, ktb_ref, ifb_ref))
            sub = k if direction == 0 else n_sub - 1 - k
            tok = slice(sub * length, (sub + 1) * length)
            earlier = (s_idx <= t_idx) if direction == 0 else (s_idx >= t_idx)
            pre = if_ref[:, tok]
            log_f = _log_sigmoid(pre[2 * nh:4 * nh, :])
            tri = jnp.where((t_idx <= s_idx) if direction == 0 else (t_idx >= s_idx), 1.0, 0.0)
            b_rows = jnp.dot(log_f, tri, precision=lax.Precision.HIGHEST, preferred_element_type=F32)
            last = length - 1 if direction == 0 else 0
            for head in range(nh):
                j = direction * nh + head
                b_row = b_rows[j:j + 1, :]
                a_row = pre[j:j + 1, :] - b_row
                b_tot = b_row[:, last:last + 1]
                m_state = m_carry[j]
                g_row = b_tot + a_row
                m_new = jnp.maximum(b_tot + m_state, jnp.max(g_row, axis=-1, keepdims=True))
                m_carry[j] = m_new
                chains.append(dict(
                    k=k, j=j, sl=slice(head * M_DK, (head + 1) * M_DK), tok=tok,
                    q_ref=q_ref, v_ref=v_ref, kt_ref=kt_ref, earlier=earlier,
                    h_ref=hf_ref if direction == 0 else hb_ref,
                    f_row=log_f[j:j + 1, :], a_row=a_row, m_state=m_state,
                    decay=jnp.exp(b_tot + m_state - m_new), k_weight=jnp.exp(g_row - m_new)))

    for ch in chains:
        a_mask = jnp.where(ch["earlier"], ch["a_row"], NEG)
        m_col = jnp.maximum(ch["m_state"], jnp.max(a_mask, axis=-1, keepdims=True))
        b_col = jnp.sum(jnp.where(ch["earlier"], ch["f_row"], 0.0), axis=-1, keepdims=True)
        ch["a_mask"] = a_mask
        ch["m_rep"] = jnp.broadcast_to(m_col, (length, M_DV))
        ch["mt_rep"] = jnp.broadcast_to(b_col + m_col, (length, M_DV))
    for ch in chains:
        q = ch["q_ref"][ch["tok"], ch["sl"]]
        s = _dot(q, ch["kt_ref"][ch["sl"], ch["tok"]]) * jnp.exp(ch.pop("a_mask") - ch["m_rep"])
        ch["q"] = q
        ch["s"] = s.astype(BF16)
    for k in range(n_sub):
        group = [ch for ch in chains if ch["k"] == k]
        for ch in group:
            v_aug = jnp.concatenate([ch["v_ref"][ch["tok"], ch["sl"]], ones], axis=-1)
            w_int = jnp.exp(ch["m_state"] - ch["m_rep"])
            inter = _dot(ch.pop("q"), c_scr[ch["j"]].astype(BF16))
            intra = _dot(ch.pop("s"), v_aug)
            num = w_int * inter[:, :M_DV] + intra[:, :M_DV]
            den = w_int * inter[:, M_DV:] + intra[:, M_DV:]
            ch["h_ref"][ch["tok"], ch["sl"]] = num / jnp.maximum(jnp.abs(den), jnp.exp(-ch["mt_rep"]))
            ch["v_aug"] = v_aug
        for ch in group:
            j = ch["j"]
            kw = (ch["kt_ref"][ch["sl"], ch["tok"]].astype(F32) * ch["k_weight"]).astype(BF16)
            c_scr[j] = ch["decay"] * c_scr[j] + _dot(kw, ch.pop("v_aug"))
    for j in range(2 * nh):
        m_scr[j:j + 1, :] = jnp.broadcast_to(m_carry[j], (1, LANES))

    @pl.when(step == pl.num_programs(1) - 1)
    def _():
        c_out_ref[0] = c_scr[...]
        m_out_ref[0] = m_scr[...]


def _bidir_mlstm(mqv, mkt, ift, c0, m0, batch, n):
    span = min(M_CHUNKS_PER_STEP * M_CHUNK, n)
    assert n % span == 0 and span % M_CHUNK == 0
    nc = n // span
    nh2 = 2 * M_HEADS
    fwd = lambda b, c: b * nc + c
    bwd = lambda b, c: b * nc + nc - 1 - c
    tok = lambda j, at: pl.BlockSpec((span, M_WIDTH), lambda b, c: (at(b, c), j))
    feat = lambda rows, at: pl.BlockSpec((rows, span), lambda b, c: (0, at(b, c)))
    c_spec = pl.BlockSpec((1, nh2, M_DK, 2 * M_DV), lambda b, c: (b, 0, 0, 0))
    m_spec = pl.BlockSpec((1, nh2, LANES), lambda b, c: (b, 0, 0))
    stream = lambda at: [tok(0, at), tok(1, at), feat(M_WIDTH, at), feat(2 * nh2, at)]
    return pl.pallas_call(
        _mlstm_kernel,
        grid=(batch, nc),
        in_specs=stream(fwd) + stream(bwd) + [c_spec, m_spec],
        out_specs=[tok(0, fwd), tok(0, bwd), c_spec, m_spec],
        out_shape=[
            jax.ShapeDtypeStruct((batch * n, M_WIDTH), F32),
            jax.ShapeDtypeStruct((batch * n, M_WIDTH), F32),
            jax.ShapeDtypeStruct(c0.shape, F32),
            jax.ShapeDtypeStruct(m0.shape, F32),
        ],
        scratch_shapes=[
            pltpu.VMEM((nh2, M_DK, 2 * M_DV), F32),
            pltpu.VMEM((nh2, LANES), F32),
        ],
        compiler_params=_params("parallel", "arbitrary"),
        name="bidir_mlstm",
    )(mqv, mqv, mkt, ift, mqv, mqv, mkt, ift, c0, m0)


def _merge_kernel(x_ref, mod_ref, att_ref, hf_ref, hb_ref, cvb_ref, u_ref, uprev_ref, unext_ref,
                  gpre_ref, gpost_ref, mng_ref, convw_ref, bgate_ref,
                  wmo_ref, wgate_ref, wna_ref, wm_ref, wcv_ref, wout_ref,
                  o_ref, *, seq_len):
    tile, d = x_ref.shape
    x = x_ref[...]
    mod = mod_ref[0]
    h = _rmsnorm(x, gpre_ref[...]) * (1.0 + mod[1:2]) + mod[0:1]
    hb16 = h.astype(BF16)

    m_o = _dot(hb16, wmo_ref[...])
    hm_parts = []
    for head in range(M_HEADS):
        sl = slice(head * M_DV, (head + 1) * M_DV)
        hh = hf_ref[:, sl] + hb_ref[:, sl]
        hm_parts.append(hh * lax.rsqrt(jnp.mean(hh * hh, axis=-1, keepdims=True) + EPS))
    hm = jnp.concatenate(hm_parts, axis=-1) * mng_ref[...] * _sigmoid(m_o)

    u = u_ref[...]
    row = lax.broadcasted_iota(jnp.int32, (tile, 1), 0)
    pos = (pl.program_id(0) * tile + row) % seq_len
    u_before = jnp.where(row == 0, uprev_ref[SUBLANES - 1:SUBLANES, :], pltpu.roll(u, 1, 0))
    u_after = jnp.where(row == tile - 1, unext_ref[0:1, :], pltpu.roll(u, tile - 1, 0))
    u_before = jnp.where(pos == 0, 0.0, u_before)
    u_after = jnp.where(pos == seq_len - 1, 0.0, u_after)
    cw = convw_ref[...]
    hc = cvb_ref[...] * (u_before * cw[0:1] + u * cw[1:2] + u_after * cw[2:3])

    bg = bgate_ref[...]
    merged = (_sigmoid(_dot(hb16, wgate_ref[:, 0:d]) + bg[:, 0:d]) * _dot(att_ref[...], wna_ref[...])
              + _sigmoid(_dot(hb16, wgate_ref[:, d:2 * d]) + bg[:, d:2 * d]) * _dot(hm.astype(BF16), wm_ref[...])
              + _sigmoid(_dot(hb16, wgate_ref[:, 2 * d:3 * d]) + bg[:, 2 * d:3 * d])
              * _dot(hc.astype(BF16), wcv_ref[...]))
    y = _dot(merged.astype(BF16), wout_ref[...])
    o_ref[...] = x + mod[2:3] * _rmsnorm(y, gpost_ref[...])


def _merge(x, mod, att, hf, hb, cvb, u, consts, *, seq_len):
    ntok, d = x.shape
    tile = min(TOKEN_TILE, ntok)
    per_seq = mod.shape[0] > 1
    assert ntok % tile == 0 and (not per_seq or seq_len % tile == 0)
    assert tile % seq_len == 0 or seq_len % tile == 0
    row = lambda n: pl.BlockSpec((tile, n), lambda i: (i, 0))
    halo = tile // SUBLANES
    last_halo = ntok // SUBLANES - 1
    const_ops, const_specs = zip(*[_resident(c) for c in consts])
    return pl.pallas_call(
        functools.partial(_merge_kernel, seq_len=seq_len),
        grid=(ntok // tile,),
        in_specs=[
            row(d),
            pl.BlockSpec((1, N_MOD, d), _mod_index_map(per_seq, max(seq_len // tile, 1))),
            row(NA_WIDTH), row(M_WIDTH), row(M_WIDTH), row(CONV_WIDTH), row(CONV_WIDTH),
            pl.BlockSpec((SUBLANES, CONV_WIDTH), lambda i: (jnp.maximum(i * halo - 1, 0), 0)),
            pl.BlockSpec((SUBLANES, CONV_WIDTH), lambda i: (jnp.minimum((i + 1) * halo, last_halo), 0)),
        ] + list(const_specs),
        out_specs=row(d),
        out_shape=jax.ShapeDtypeStruct((ntok, d), F32),
        compiler_params=_params("parallel"),
        name="branch_merge",
    )(x, mod, att, hf, hb, cvb, u, u, u, *const_ops)


def _ffn_kernel(x_ref, mod_ref, gpre_ref, gpost_ref, wg_ref, wu_ref, wd_ref, o_ref):
    mod = mod_ref[0]
    d_ff = wg_ref.shape[1]
    chunk = d_ff // FFN_CHUNKS
    rows = x_ref.shape[0] // FFN_ROW_GROUPS
    groups = [slice(g * rows, (g + 1) * rows) for g in range(FFN_ROW_GROUPS)]
    hs = [(_rmsnorm(x_ref[g, :], gpre_ref[...]) * (1.0 + mod[4:5]) + mod[3:4]).astype(BF16) for g in groups]
    ffs = []
    for hb16 in hs:
        ff = None
        for c in range(FFN_CHUNKS):
            sl = slice(c * chunk, (c + 1) * chunk)
            gate = _dot(hb16, wg_ref[:, sl])
            act = (gate * _sigmoid(gate) * _dot(hb16, wu_ref[:, sl])).astype(BF16)
            part = _dot(act, wd_ref[sl, :])
            ff = part if ff is None else ff + part
        ffs.append(ff)
    for g, ff in zip(groups, ffs):
        o_ref[g, :] = x_ref[g, :] + mod[5:6] * _rmsnorm(ff, gpost_ref[...])


def _ffn(x, mod, g_pre, g_post, wg, wu, wd, *, seq_len):
    ntok, d = x.shape
    tile = min(TOKEN_TILE, ntok)
    per_seq = mod.shape[0] > 1
    assert ntok % tile == 0 and (not per_seq or seq_len % tile == 0)
    const_ops, const_specs = zip(*[_resident(c) for c in (g_pre, g_post, wg, wu, wd)])
    assert const_ops[2].shape[-1] % (FFN_CHUNKS * 2 * LANES) == 0
    row = pl.BlockSpec((tile, d), lambda i: (i, 0))
    return pl.pallas_call(
        _ffn_kernel,
        grid=(ntok // tile,),
        in_specs=[row, pl.BlockSpec((1, N_MOD, d), _mod_index_map(per_seq, max(seq_len // tile, 1)))]
        + list(const_specs),
        out_specs=row,
        out_shape=jax.ShapeDtypeStruct((ntok, d), F32),
        compiler_params=_params("parallel"),
        name="swiglu_ffn",
    )(x, mod, *const_ops)


def _layer_weights(l, w_in, b_gate, m_b_i, m_b_f, g_pre_mix, g_post_mix, g_pre_ffn, g_post_ffn, m_norm_g, conv_w,
                   w_br_na, w_br_m, w_br_cv, w_out, w_ffn_gate, w_ffn_up, w_ffn_down):
    d = w_in.shape[1]
    o_mo = 3 * NA_WIDTH + 3 * M_WIDTH
    o_if = o_mo + M_WIDTH
    o_cv = o_if + 4 * M_HEADS
    o_gate = o_cv + 3 * CONV_WIDTH
    assert w_in.shape[2] == o_gate + N_BRANCH * d
    o_mk = 3 * NA_WIDTH + M_WIDTH
    o_mv = o_mk + M_WIDTH
    cols = lambda a, b: w_in[l, :, a:b].astype(BF16)
    bift = jnp.concatenate([m_b_i[l].reshape(-1), m_b_f[l].reshape(-1)]).reshape(4 * M_HEADS, 1)
    vec = lambda a: a[l].reshape(1, -1)
    return dict(
        wa=jnp.concatenate([cols(0, o_mk), cols(o_mv, o_mo)], axis=1),
        wkt=cols(o_mk, o_mv).T, wift=cols(o_if, o_cv).T, bift=bift, wcv=cols(o_cv, o_gate),
        g_pre_mix=vec(g_pre_mix), g_pre_ffn=vec(g_pre_ffn), g_post_ffn=vec(g_post_ffn),
        merge_consts=(vec(g_pre_mix), vec(g_post_mix), vec(m_norm_g), conv_w[l], vec(b_gate),
                      cols(o_mo, o_if), cols(o_gate, o_gate + N_BRANCH * d),
                      (w_br_na, l), (w_br_m, l), (w_br_cv, l), (w_out, l)),
        wg=(w_ffn_gate, l), wu=(w_ffn_up, l), wd=(w_ffn_down, l),
    )


def _trunk_layer(x, mod, lw, batch, seq_len, ctx):
    is_ctx = ctx is None
    naqkv, mqv, mkt, ift, cvb, u = _inproj(x, mod, lw["g_pre_mix"], lw["wa"], lw["wkt"], lw["wift"], lw["bift"],
                                           lw["wcv"], seq_len=seq_len, na_dtype=F32 if is_ctx else BF16)
    if is_ctx:
        att = _context_attention(naqkv, batch, seq_len)
        c0 = jnp.zeros((batch, 2 * M_HEADS, M_DK, 2 * M_DV), F32)
        m0 = jnp.zeros((batch, 2 * M_HEADS, LANES), F32)
    else:
        k_ctx, v_ctx, bias, c0, m0 = ctx
        att = _neighborhood_attention(naqkv, k_ctx, v_ctx, bias, batch, seq_len)
    hf, hb, c_new, m_new = _bidir_mlstm(mqv, mkt, ift, c0, m0, batch, seq_len)
    x = _merge(x, mod, att, hf, hb, cvb, u, lw["merge_consts"], seq_len=seq_len)
    x = _ffn(x, mod, lw["g_pre_ffn"], lw["g_post_ffn"], lw["wg"], lw["wu"], lw["wd"], seq_len=seq_len)
    return x, (naqkv, c_new, m_new)


def kernel(x_prompt, x_sample, c, cache_k, cache_v, state_C, state_n, state_m, c_ctx, w_ada, b_ada, g_pre_mix,
           g_post_mix, g_pre_ffn, g_post_ffn, w_in, b_gate, m_b_i, m_b_f, na_rpb, m_norm_g, conv_w, w_br_na,
           w_br_m, w_br_cv, w_out, w_ffn_gate, w_ffn_up, w_ffn_down):
    batch, seq, d = x_prompt.shape
    dec_batch, dec_seq, _ = x_sample.shape
    depth = w_in.shape[0]
    past = cache_k.shape[3]
    assert dec_batch + 1 <= COND_ROWS

    cond = jnp.zeros((COND_ROWS, d), F32).at[:dec_batch].set(c).at[dec_batch].set(c_ctx)
    mod = _ada_modulation(cond, w_ada, b_ada).reshape(depth, COND_ROWS, N_MOD, d)

    xp = x_prompt.reshape(batch * seq, d)
    xs = x_sample.reshape(dec_batch * dec_seq, d)
    ks, vs, cs, ns, ms = [], [], [], [], []
    w_br_na, w_br_m, w_br_cv, w_out, w_ffn_gate, w_ffn_up, w_ffn_down = (
        w.astype(BF16) for w in (w_br_na, w_br_m, w_br_cv, w_out, w_ffn_gate, w_ffn_up, w_ffn_down))
    for l in range(depth):
        lw = _layer_weights(l, w_in, b_gate, m_b_i, m_b_f, g_pre_mix, g_post_mix, g_pre_ffn, g_post_ffn,
                            m_norm_g, conv_w, w_br_na, w_br_m, w_br_cv, w_out, w_ffn_gate, w_ffn_up, w_ffn_down)
        xp, (naqkv, c_l, m_l) = _trunk_layer(xp, mod[l, dec_batch:dec_batch + 1], lw, batch, seq, None)
        heads = lambda a: a.reshape(batch, seq, NA_HEADS, NA_HEAD_DIM).transpose(0, 2, 1, 3)
        ks.append(heads(naqkv[:, NA_WIDTH:2 * NA_WIDTH]))
        vs.append(heads(naqkv[:, 2 * NA_WIDTH:]))
        cs.append(c_l[..., :M_DV].reshape(batch, 2, M_HEADS, M_DK, M_DV))
        ns.append(c_l[..., M_DV].reshape(batch, 2, M_HEADS, M_DK))
        ms.append(m_l[:, :, 0].reshape(batch, 2, M_HEADS))

        tokens = lambda a: a.transpose(0, 2, 1, 3).reshape(dec_batch, past, NA_WIDTH).astype(BF16)
        c0 = jnp.concatenate([state_C[:, l], jnp.broadcast_to(state_n[:, l][..., None], state_C[:, l].shape)],
                             axis=-1).reshape(dec_batch, 2 * M_HEADS, M_DK, 2 * M_DV)
        m0 = jnp.broadcast_to(state_m[:, l].reshape(dec_batch, 2 * M_HEADS, 1), (dec_batch, 2 * M_HEADS, LANES))
        ctx = (tokens(cache_k[:, l]), tokens(cache_v[:, l]), _na_bias_tables(na_rpb[l], dec_seq // GRID_W), c0, m0)
        xs, _ = _trunk_layer(xs, mod[l, :dec_batch], lw, dec_batch, dec_seq, ctx)

    return (xp.reshape(batch, seq, d), xs.reshape(dec_batch, dec_seq, d),
            jnp.stack(ks, axis=1), jnp.stack(vs, axis=1), jnp.stack(cs, axis=1),
            jnp.stack(ns, axis=1), jnp.stack(ms, axis=1))
```

```python
import functools

import numpy as np
import jax
import jax.numpy as jnp
from jax import lax
from jax.experimental import pallas as pl
from jax.experimental.pallas import tpu as pltpu

F32 = jnp.float32
BF16 = jnp.bfloat16

GRID_W = 64
NA_HEADS = 8
NA_HEAD_DIM = 64
NA_WIDTH = NA_HEADS * NA_HEAD_DIM
NA_KH = 8
NA_KW = 16
NA_SCALE = NA_HEAD_DIM ** -0.5
M_HEADS = 4
M_DK = 128
M_DV = 128
M_WIDTH = M_HEADS * M_DV
M_CHUNK = 128
M_SCALE = M_DK ** -0.5
CONV_WIDTH = 512
N_BRANCH = 3
N_MOD = 6
EPS = 1e-6
NEG = -1e30

LANES = 128
SUBLANES = 8
VMEM_LIMIT_BYTES = 56 * 1024 * 1024

TOKEN_TILE = 512
NA_ROWS_PER_BLOCK = 4
M_CHUNKS_PER_STEP = 8
FFN_ROW_GROUPS = 2
FFN_CHUNKS = 1
ADA_TILE = 1024
COND_ROWS = 16


def _const_spec(shape):
    nd = len(shape)
    return pl.BlockSpec(shape, lambda *_: (0,) * nd, pipeline_mode=pl.Buffered(1))


def _resident(w):
    if isinstance(w, tuple):
        stacked, layer = w
        nd = stacked.ndim - 1
        return stacked, pl.BlockSpec((None,) + stacked.shape[1:], lambda *_: (layer,) + (0,) * nd,
                                     pipeline_mode=pl.Buffered(1))
    return w, _const_spec(w.shape)


def _params(*sem):
    return pltpu.CompilerParams(dimension_semantics=sem, vmem_limit_bytes=VMEM_LIMIT_BYTES)


def _sigmoid(x):
    return 0.5 * jnp.tanh(0.5 * x) + 0.5


def _log_sigmoid(x):
    return jnp.minimum(x, 0.0) - jnp.log1p(jnp.exp(-jnp.abs(x)))


def _rmsnorm(x, g):
    return x * lax.rsqrt(jnp.mean(x * x, axis=-1, keepdims=True) + EPS) * g


def _dot(a, b):
    return jnp.dot(a, b, preferred_element_type=F32)


def _dot_nt(a, b):
    return lax.dot_general(a, b, (((1,), (1,)), ((), ())), preferred_element_type=F32)


def _ada_kernel(cond_ref, w_ref, b_ref, o_ref):
    c = cond_ref[...]
    s = (c * _sigmoid(c)).astype(BF16)
    o_ref[0] = _dot(s, w_ref[0].astype(BF16)) + b_ref[0]


def _ada_modulation(cond, w_ada, b_ada):
    depth, d, n = w_ada.shape
    return pl.pallas_call(
        _ada_kernel,
        grid=(depth, n // ADA_TILE),
        in_specs=[
            pl.BlockSpec((COND_ROWS, d), lambda l, j: (0, 0)),
            pl.BlockSpec((1, d, ADA_TILE), lambda l, j: (l, 0, j)),
            pl.BlockSpec((1, 1, ADA_TILE), lambda l, j: (l, 0, j)),
        ],
        out_specs=pl.BlockSpec((1, COND_ROWS, ADA_TILE), lambda l, j: (l, 0, j)),
        out_shape=jax.ShapeDtypeStruct((depth, COND_ROWS, n), F32),
        compiler_params=_params("arbitrary", "arbitrary"),
        name="ada_modulation",
    )(cond, w_ada, b_ada.reshape(depth, 1, n))


def _mod_index_map(per_seq, tiles_per_seq):
    if per_seq:
        return lambda i: (i // tiles_per_seq, 0, 0)
    return lambda i: (0, 0, 0)


def _inproj_kernel(x_ref, mod_ref, g_ref, wa_ref, wkt_ref, wift_ref, bift_ref, wcv_ref,
                   naqkv_ref, mqv_ref, mkt_ref, ift_ref, cvb_ref, u_ref):
    mod = mod_ref[0]
    h = _rmsnorm(x_ref[...], g_ref[...]) * (1.0 + mod[1:2]) + mod[0:1]
    hb = h.astype(BF16)
    w = NA_WIDTH
    naqkv_ref[:, 0:w] = (_dot(hb, wa_ref[:, 0:w]) * NA_SCALE).astype(naqkv_ref.dtype)
    naqkv_ref[:, w:2 * w] = _dot(hb, wa_ref[:, w:2 * w]).astype(naqkv_ref.dtype)
    naqkv_ref[:, 2 * w:3 * w] = _dot(hb, wa_ref[:, 2 * w:3 * w]).astype(naqkv_ref.dtype)
    o = 3 * w
    m = M_WIDTH
    mqv_ref[:, 0:m] = (_dot(hb, wa_ref[:, o:o + m]) * M_SCALE).astype(BF16)
    mqv_ref[:, m:2 * m] = _dot(hb, wa_ref[:, o + m:o + 2 * m]).astype(BF16)
    mkt_ref[...] = _dot_nt(wkt_ref[...], hb).astype(BF16)
    ift_ref[...] = _dot_nt(wift_ref[...], hb) + bift_ref[...]
    c = CONV_WIDTH
    cvb_ref[...] = _dot(hb, wcv_ref[:, 0:c])
    u_ref[...] = _dot(hb, wcv_ref[:, c:2 * c]) * _dot(hb, wcv_ref[:, 2 * c:3 * c])


def _inproj(x, mod, g_pre, wa, wkt, wift, bift, wcv, *, seq_len, na_dtype):
    ntok, d = x.shape
    tile = min(TOKEN_TILE, ntok)
    per_seq = mod.shape[0] > 1
    assert ntok % tile == 0 and (not per_seq or seq_len % tile == 0)
    row = lambda n: pl.BlockSpec((tile, n), lambda i: (i, 0))
    col = lambda n: pl.BlockSpec((n, tile), lambda i: (0, i))
    n_gate = 4 * M_HEADS
    return pl.pallas_call(
        _inproj_kernel,
        grid=(ntok // tile,),
        in_specs=[
            row(d),
            pl.BlockSpec((1, N_MOD, d), _mod_index_map(per_seq, max(seq_len // tile, 1))),
            _const_spec(g_pre.shape),
            _const_spec(wa.shape),
            _const_spec(wkt.shape),
            _const_spec(wift.shape),
            _const_spec(bift.shape),
            _const_spec(wcv.shape),
        ],
        out_specs=[row(3 * NA_WIDTH), row(2 * M_WIDTH), col(M_WIDTH), col(n_gate),
                   row(CONV_WIDTH), row(CONV_WIDTH)],
        out_shape=[
            jax.ShapeDtypeStruct((ntok, 3 * NA_WIDTH), na_dtype),
            jax.ShapeDtypeStruct((ntok, 2 * M_WIDTH), BF16),
            jax.ShapeDtypeStruct((M_WIDTH, ntok), BF16),
            jax.ShapeDtypeStruct((n_gate, ntok), F32),
            jax.ShapeDtypeStruct((ntok, CONV_WIDTH), F32),
            jax.ShapeDtypeStruct((ntok, CONV_WIDTH), F32),
        ],
        compiler_params=_params("parallel"),
        name="in_projection",
    )(x, mod, g_pre, wa, wkt, wift, bift, wcv)


def _softmax_pv(scores, values):
    mx = functools.reduce(jnp.maximum, [jnp.max(s, axis=-1, keepdims=True) for s in scores])
    es = [jnp.exp(s - mx) for s in scores]
    den = functools.reduce(jnp.add, [jnp.sum(e, axis=-1, keepdims=True) for e in es])
    acc = functools.reduce(jnp.add, [_dot(e.astype(BF16), v) for e, v in zip(es, values)])
    return acc / den


def _ctx_attn_kernel(q_ref, k_ref, v_ref, o_ref):
    low = lax.broadcasted_iota(jnp.int32, (1, LANES), 1) < NA_HEAD_DIM
    for p in range(NA_WIDTH // LANES):
        sl = slice(p * LANES, (p + 1) * LANES)
        q2 = q_ref[0, :, sl].astype(BF16)
        k2 = k_ref[0, :, sl].astype(BF16)
        v2 = v_ref[0, :, sl].astype(BF16)
        halves = []
        for first in (True, False):
            qm = jnp.where(low == first, q2, jnp.zeros_like(q2))
            halves.append(_softmax_pv([_dot_nt(qm, k2)], [v2]))
        o_ref[0, :, sl] = jnp.where(low, halves[0], halves[1]).astype(o_ref.dtype)


def _context_attention(naqkv, batch, seq):
    a = naqkv.reshape(batch, seq, 3 * NA_WIDTH)
    spec = lambda j: pl.BlockSpec((1, seq, NA_WIDTH), lambda b: (b, 0, j))
    out = pl.pallas_call(
        _ctx_attn_kernel,
        grid=(batch,),
        in_specs=[spec(0), spec(1), spec(2)],
        out_specs=spec(0),
        out_shape=jax.ShapeDtypeStruct((batch, seq, NA_WIDTH), BF16),
        compiler_params=_params("parallel"),
        name="context_attention",
    )(a, a, a)
    return out.reshape(batch * seq, NA_WIDTH)


def _na_geometry(rows):
    r = NA_ROWS_PER_BLOCK
    slab = r + NA_KH - 1
    assert rows % r == 0 and rows >= slab and rows >= NA_KH
    nblk = rows // r
    pats = []
    for i in range(nblk):
        r0 = r * i
        s0 = int(np.clip(r0 - NA_KH // 2, 0, rows - slab))
        q_rows = r0 + np.arange(r)
        k_rows = s0 + np.arange(slab)
        rs = np.clip(q_rows - NA_KH // 2, 0, rows - NA_KH)
        valid = (k_rows[None, :] >= rs[:, None]) & (k_rows[None, :] < rs[:, None] + NA_KH)
        dr = np.clip(k_rows[None, :] - q_rows[:, None] + NA_KH - 1, 0, 2 * NA_KH - 2)
        pats.append((valid, dr))
    same = lambda a, b: np.array_equal(a[0], b[0]) and np.array_equal(a[1][a[0]], b[1][b[0]])
    assert nblk >= 3 and all(same(pats[1], p) for p in pats[1:-1])
    return r, slab, nblk, [pats[0], pats[1], pats[-1]]


def _na_bias_tables(rpb, rows):
    r, slab, _, pats = _na_geometry(rows)
    c = np.arange(GRID_W)
    cs = np.clip(c - NA_KW // 2, 0, GRID_W - NA_KW)
    kc = np.arange(GRID_W)
    col_ok = (kc[None, :] >= cs[:, None]) & (kc[None, :] < cs[:, None] + NA_KW)
    pad = GRID_W - NA_KW
    assert np.all(np.abs(kc[None, :] - c[:, None])[col_ok] <= NA_KW - 1)
    padded = jnp.pad(rpb.astype(F32), ((0, 0), (0, 0), (pad, pad)))
    toeplitz = jnp.stack([padded[:, :, GRID_W - 1 - ci:2 * GRID_W - 1 - ci] for ci in range(GRID_W)], axis=2)
    t4 = jnp.where(col_ok[None, None], toeplitz, NEG)
    masked = jnp.full((rpb.shape[0], GRID_W, GRID_W), NEG, F32)
    out = []
    for valid, dr in pats:
        tile_rows = [jnp.concatenate([t4[:, dr[i, j]] if valid[i, j] else masked for j in range(slab)], axis=-1)
                     for i in range(r)]
        out.append(jnp.concatenate(tile_rows, axis=1))
    return jnp.stack(out)


def _na_kernel(q_ref, k_ref, v_ref, kc_ref, vc_ref, bias_ref, o_ref, *, rows):
    r, slab, _, _ = _na_geometry(rows)
    i = pl.program_id(1)
    s0 = jnp.clip(r * i - NA_KH // 2, 0, rows - slab)
    start = pl.multiple_of(s0 * GRID_W, GRID_W)
    n_slab = slab * GRID_W
    low = lax.broadcasted_iota(jnp.int32, (1, LANES), 1) < NA_HEAD_DIM
    n_pairs = NA_WIDTH // LANES
    lanes = lambda p: slice(p * LANES, (p + 1) * LANES)

    def scores(head):
        p, half = divmod(head, 2)
        q2 = q_ref[0, :, lanes(p)]
        qm = jnp.where(low == (half == 0), q2, jnp.zeros_like(q2))
        s_loc = _dot_nt(qm, k_ref[0, pl.ds(start, n_slab), lanes(p)]) + bias_ref[0, head]
        s_ctx = _dot_nt(qm, kc_ref[0, :, lanes(p)])
        return s_loc, s_ctx

    def weights(s_loc, s_ctx):
        mx = jnp.maximum(jnp.max(s_loc, axis=-1, keepdims=True), jnp.max(s_ctx, axis=-1, keepdims=True))
        return jnp.exp(s_loc - mx).astype(BF16), jnp.exp(s_ctx - mx).astype(BF16)

    def attend(head, e_loc, e_ctx):
        p = head // 2
        ones = lambda n: jnp.ones((n, LANES), BF16)
        acc = (_dot(e_loc, jnp.concatenate([v_ref[0, pl.ds(start, n_slab), lanes(p)], ones(n_slab)], axis=-1))
               + _dot(e_ctx, jnp.concatenate([vc_ref[0, :, lanes(p)], ones(vc_ref.shape[1])], axis=-1)))
        return acc[:, :LANES] / acc[:, LANES:]

    sc, ex, out = {}, {}, {}
    for step in range(NA_HEADS + 2):
        if step < NA_HEADS:
            sc[step] = scores(step)
        if 0 <= step - 1 < NA_HEADS:
            ex[step - 1] = weights(*sc.pop(step - 1))
        if 0 <= step - 2 < NA_HEADS:
            head = step - 2
            out[head] = attend(head, *ex.pop(head))
            if head % 2 == 1:
                o_ref[0, :, lanes(head // 2)] = jnp.where(low, out.pop(head - 1), out.pop(head)).astype(o_ref.dtype)
    assert n_pairs * 2 == NA_HEADS


def _neighborhood_attention(naqkv, k_ctx, v_ctx, bias, batch, n):
    rows = n // GRID_W
    r, slab, nblk, _ = _na_geometry(rows)
    a = naqkv.reshape(batch, n, 3 * NA_WIDTH)
    past = k_ctx.shape[1]
    qb = r * GRID_W
    whole = lambda j: pl.BlockSpec((1, n, NA_WIDTH), lambda b, i: (b, 0, j))
    ctx = pl.BlockSpec((1, past, NA_WIDTH), lambda b, i: (b, 0, 0))
    pattern = lambda b, i: (jnp.where(i == 0, 0, jnp.where(i == nblk - 1, 2, 1)), 0, 0, 0)
    out = pl.pallas_call(
        functools.partial(_na_kernel, rows=rows),
        grid=(batch, nblk),
        in_specs=[
            pl.BlockSpec((1, qb, NA_WIDTH), lambda b, i: (b, i, 0)),
            whole(1), whole(2), ctx, ctx,
            pl.BlockSpec((1, NA_HEADS, qb, slab * GRID_W), pattern),
        ],
        out_specs=pl.BlockSpec((1, qb, NA_WIDTH), lambda b, i: (b, i, 0)),
        out_shape=jax.ShapeDtypeStruct((batch, n, NA_WIDTH), BF16),
        compiler_params=_params("parallel", "arbitrary"),
        name="neighborhood_attention",
    )(a, a, a, k_ctx, v_ctx, bias)
    return out.reshape(batch * n, NA_WIDTH)


def _mlstm_kernel(qf_ref, vf_ref, ktf_ref, iff_ref, qb_ref, vb_ref, ktb_ref, ifb_ref, c0_ref, m0_ref,
                  hf_ref, hb_ref, c_out_ref, m_out_ref, c_scr, m_scr):
    step = pl.program_id(1)
    length = M_CHUNK
    nh = M_HEADS

    @pl.when(step == 0)
    def _():
        c_scr[...] = c0_ref[0]
        m_scr[...] = m0_ref[0]

    t_idx = lax.broadcasted_iota(jnp.int32, (length, length), 0)
    s_idx = lax.broadcasted_iota(jnp.int32, (length, length), 1)
    ones = jnp.ones((length, M_DV), BF16)

    n_sub = qf_ref.shape[0] // length
    chains = []
    m_carry = [m_scr[j:j + 1, 0:1] for j in range(2 * nh)]
    for k in range(n_sub):
        for direction in range(2):
            q_ref, v_ref, kt_ref, if_ref = ((qf_ref, vf_ref, ktf_ref, iff_ref) if direction == 0
                                            else (qb_ref, vb_ref, ktb_ref, ifb_ref))
            sub = k if direction == 0 else n_sub - 1 - k
            tok = slice(sub * length, (sub + 1) * length)
            earlier = (s_idx <= t_idx) if direction == 0 else (s_idx >= t_idx)
            pre = if_ref[:, tok]
            log_f = _log_sigmoid(pre[2 * nh:4 * nh, :])
            tri = jnp.where((t_idx <= s_idx) if direction == 0 else (t_idx >= s_idx), 1.0, 0.0)
            b_rows = jnp.dot(log_f, tri, precision=lax.Precision.HIGHEST, preferred_element_type=F32)
            last = length - 1 if direction == 0 else 0
            for head in range(nh):
                j = direction * nh + head
                b_row = b_rows[j:j + 1, :]
                a_row = pre[j:j + 1, :] - b_row
                b_tot = b_row[:, last:last + 1]
                m_state = m_carry[j]
                g_row = b_tot + a_row
                m_new = jnp.maximum(b_tot + m_state, jnp.max(g_row, axis=-1, keepdims=True))
                m_carry[j] = m_new
                chains.append(dict(
                    k=k, j=j, sl=slice(head * M_DK, (head + 1) * M_DK), tok=tok,
                    q_ref=q_ref, v_ref=v_ref, kt_ref=kt_ref, earlier=earlier,
                    h_ref=hf_ref if direction == 0 else hb_ref,
                    f_row=log_f[j:j + 1, :], a_row=a_row, m_state=m_state,
                    decay=jnp.exp(b_tot + m_state - m_new), k_weight=jnp.exp(g_row - m_new)))

    for ch in chains:
        a_mask = jnp.where(ch["earlier"], ch["a_row"], NEG)
        m_col = jnp.maximum(ch["m_state"], jnp.max(a_mask, axis=-1, keepdims=True))
        b_col = jnp.sum(jnp.where(ch["earlier"], ch["f_row"], 0.0), axis=-1, keepdims=True)
        ch["a_mask"] = a_mask
        ch["m_rep"] = jnp.broadcast_to(m_col, (length, M_DV))
        ch["mt_rep"] = jnp.broadcast_to(b_col + m_col, (length, M_DV))
    for ch in chains:
        q = ch["q_ref"][ch["tok"], ch["sl"]]
        s = _dot(q, ch["kt_ref"][ch["sl"], ch["tok"]]) * jnp.exp(ch.pop("a_mask") - ch["m_rep"])
        ch["q"] = q
        ch["s"] = s.astype(BF16)
    for k in range(n_sub):
        group = [ch for ch in chains if ch["k"] == k]
        for ch in group:
            v_aug = jnp.concatenate([ch["v_ref"][ch["tok"], ch["sl"]], ones], axis=-1)
            w_int = jnp.exp(ch["m_state"] - ch["m_rep"])
            inter = _dot(ch.pop("q"), c_scr[ch["j"]].astype(BF16))
            intra = _dot(ch.pop("s"), v_aug)
            num = w_int * inter[:, :M_DV] + intra[:, :M_DV]
            den = w_int * inter[:, M_DV:] + intra[:, M_DV:]
            ch["h_ref"][ch["tok"], ch["sl"]] = num / jnp.maximum(jnp.abs(den), jnp.exp(-ch["mt_rep"]))
            ch["v_aug"] = v_aug
        for ch in group:
            j = ch["j"]
            kw = (ch["kt_ref"][ch["sl"], ch["tok"]].astype(F32) * ch["k_weight"]).astype(BF16)
            c_scr[j] = ch["decay"] * c_scr[j] + _dot(kw, ch.pop("v_aug"))
    for j in range(2 * nh):
        m_scr[j:j + 1, :] = jnp.broadcast_to(m_carry[j], (1, LANES))

    @pl.when(step == pl.num_programs(1) - 1)
    def _():
        c_out_ref[0] = c_scr[...]
        m_out_ref[0] = m_scr[...]


def _bidir_mlstm(mqv, mkt, ift, c0, m0, batch, n):
    span = min(M_CHUNKS_PER_STEP * M_CHUNK, n)
    assert n % span == 0 and span % M_CHUNK == 0
    nc = n // span
    nh2 = 2 * M_HEADS
    fwd = lambda b, c: b * nc + c
    bwd = lambda b, c: b * nc + nc - 1 - c
    tok = lambda j, at: pl.BlockSpec((span, M_WIDTH), lambda b, c: (at(b, c), j))
    feat = lambda rows, at: pl.BlockSpec((rows, span), lambda b, c: (0, at(b, c)))
    c_spec = pl.BlockSpec((1, nh2, M_DK, 2 * M_DV), lambda b, c: (b, 0, 0, 0))
    m_spec = pl.BlockSpec((1, nh2, LANES), lambda b, c: (b, 0, 0))
    stream = lambda at: [tok(0, at), tok(1, at), feat(M_WIDTH, at), feat(2 * nh2, at)]
    return pl.pallas_call(
        _mlstm_kernel,
        grid=(batch, nc),
        in_specs=stream(fwd) + stream(bwd) + [c_spec, m_spec],
        out_specs=[tok(0, fwd), tok(0, bwd), c_spec, m_spec],
        out_shape=[
            jax.ShapeDtypeStruct((batch * n, M_WIDTH), F32),
            jax.ShapeDtypeStruct((batch * n, M_WIDTH), F32),
            jax.ShapeDtypeStruct(c0.shape, F32),
            jax.ShapeDtypeStruct(m0.shape, F32),
        ],
        scratch_shapes=[
            pltpu.VMEM((nh2, M_DK, 2 * M_DV), F32),
            pltpu.VMEM((nh2, LANES), F32),
        ],
        compiler_params=_params("parallel", "arbitrary"),
        name="bidir_mlstm",
    )(mqv, mqv, mkt, ift, mqv, mqv, mkt, ift, c0, m0)


def _merge_kernel(x_ref, mod_ref, att_ref, hf_ref, hb_ref, cvb_ref, u_ref, uprev_ref, unext_ref,
                  gpre_ref, gpost_ref, mng_ref, convw_ref, bgate_ref,
                  wmo_ref, wgate_ref, wna_ref, wm_ref, wcv_ref, wout_ref,
                  ffn_gpre_ref, ffn_gpost_ref, wg_ref, wu_ref, wd_ref,
                  o_ref, *, seq_len):
    tile, d = x_ref.shape
    x = x_ref[...]
    mod = mod_ref[0]
    h = _rmsnorm(x, gpre_ref[...]) * (1.0 + mod[1:2]) + mod[0:1]
    hb16 = h.astype(BF16)

    m_o = _dot(hb16, wmo_ref[...])
    hm_parts = []
    for head in range(M_HEADS):
        sl = slice(head * M_DV, (head + 1) * M_DV)
        hh = hf_ref[:, sl] + hb_ref[:, sl]
        hm_parts.append(hh * lax.rsqrt(jnp.mean(hh * hh, axis=-1, keepdims=True) + EPS))
    hm = jnp.concatenate(hm_parts, axis=-1) * mng_ref[...] * _sigmoid(m_o)

    u = u_ref[...]
    row = lax.broadcasted_iota(jnp.int32, (tile, 1), 0)
    pos = (pl.program_id(0) * tile + row) % seq_len
    u_before = jnp.where(row == 0, uprev_ref[SUBLANES - 1:SUBLANES, :], pltpu.roll(u, 1, 0))
    u_after = jnp.where(row == tile - 1, unext_ref[0:1, :], pltpu.roll(u, tile - 1, 0))
    u_before = jnp.where(pos == 0, 0.0, u_before)
    u_after = jnp.where(pos == seq_len - 1, 0.0, u_after)
    cw = convw_ref[...]
    hc = cvb_ref[...] * (u_before * cw[0:1] + u * cw[1:2] + u_after * cw[2:3])

    bg = bgate_ref[...]
    merged = (_sigmoid(_dot(hb16, wgate_ref[:, 0:d]) + bg[:, 0:d]) * _dot(att_ref[...], wna_ref[...])
              + _sigmoid(_dot(hb16, wgate_ref[:, d:2 * d]) + bg[:, d:2 * d]) * _dot(hm.astype(BF16), wm_ref[...])
              + _sigmoid(_dot(hb16, wgate_ref[:, 2 * d:3 * d]) + bg[:, 2 * d:3 * d])
              * _dot(hc.astype(BF16), wcv_ref[...]))
    y = _dot(merged.astype(BF16), wout_ref[...])
    x_mid = x + mod[2:3] * _rmsnorm(y, gpost_ref[...])
    _ffn_rows(x_mid, mod, ffn_gpre_ref, ffn_gpost_ref, wg_ref, wu_ref, wd_ref, o_ref)


def _merge(x, mod, att, hf, hb, cvb, u, consts, *, seq_len):
    ntok, d = x.shape
    tile = min(TOKEN_TILE, ntok)
    per_seq = mod.shape[0] > 1
    assert ntok % tile == 0 and (not per_seq or seq_len % tile == 0)
    assert tile % seq_len == 0 or seq_len % tile == 0
    row = lambda n: pl.BlockSpec((tile, n), lambda i: (i, 0))
    halo = tile // SUBLANES
    last_halo = ntok // SUBLANES - 1
    const_ops, const_specs = zip(*[_resident(c) for c in consts])
    return pl.pallas_call(
        functools.partial(_merge_kernel, seq_len=seq_len),
        grid=(ntok // tile,),
        in_specs=[
            row(d),
            pl.BlockSpec((1, N_MOD, d), _mod_index_map(per_seq, max(seq_len // tile, 1))),
            row(NA_WIDTH), row(M_WIDTH), row(M_WIDTH), row(CONV_WIDTH), row(CONV_WIDTH),
            pl.BlockSpec((SUBLANES, CONV_WIDTH), lambda i: (jnp.maximum(i * halo - 1, 0), 0)),
            pl.BlockSpec((SUBLANES, CONV_WIDTH), lambda i: (jnp.minimum((i + 1) * halo, last_halo), 0)),
        ] + list(const_specs),
        out_specs=row(d),
        out_shape=jax.ShapeDtypeStruct((ntok, d), F32),
        compiler_params=_params("parallel"),
        name="merge_ffn",
    )(x, mod, att, hf, hb, cvb, u, u, u, *const_ops)


def _ffn_rows(x, mod, gpre_ref, gpost_ref, wg_ref, wu_ref, wd_ref, o_ref):
    d_ff = wg_ref.shape[1]
    chunk = d_ff // FFN_CHUNKS
    assert chunk % (2 * LANES) == 0
    rows = x.shape[0] // FFN_ROW_GROUPS
    groups = [slice(g * rows, (g + 1) * rows) for g in range(FFN_ROW_GROUPS)]
    hs = [(_rmsnorm(x[g, :], gpre_ref[...]) * (1.0 + mod[4:5]) + mod[3:4]).astype(BF16) for g in groups]
    ffs = []
    for hb16 in hs:
        ff = None
        for c in range(FFN_CHUNKS):
            sl = slice(c * chunk, (c + 1) * chunk)
            gate = _dot(hb16, wg_ref[:, sl])
            act = (gate * _sigmoid(gate) * _dot(hb16, wu_ref[:, sl])).astype(BF16)
            part = _dot(act, wd_ref[sl, :])
            ff = part if ff is None else ff + part
        ffs.append(ff)
    for g, ff in zip(groups, ffs):
        o_ref[g, :] = x[g, :] + mod[5:6] * _rmsnorm(ff, gpost_ref[...])


def _layer_weights(l, w_in, b_gate, m_b_i, m_b_f, g_pre_mix, g_post_mix, g_pre_ffn, g_post_ffn, m_norm_g, conv_w,
                   w_br_na, w_br_m, w_br_cv, w_out, w_ffn_gate, w_ffn_up, w_ffn_down):
    d = w_in.shape[1]
    o_mo = 3 * NA_WIDTH + 3 * M_WIDTH
    o_if = o_mo + M_WIDTH
    o_cv = o_if + 4 * M_HEADS
    o_gate = o_cv + 3 * CONV_WIDTH
    assert w_in.shape[2] == o_gate + N_BRANCH * d
    o_mk = 3 * NA_WIDTH + M_WIDTH
    o_mv = o_mk + M_WIDTH
    cols = lambda a, b: w_in[l, :, a:b].astype(BF16)
    bift = jnp.concatenate([m_b_i[l].reshape(-1), m_b_f[l].reshape(-1)]).reshape(4 * M_HEADS, 1)
    vec = lambda a: a[l].reshape(1, -1)
    return dict(
        wa=jnp.concatenate([cols(0, o_mk), cols(o_mv, o_mo)], axis=1),
        wkt=cols(o_mk, o_mv).T, wift=cols(o_if, o_cv).T, bift=bift, wcv=cols(o_cv, o_gate),
        g_pre_mix=vec(g_pre_mix), g_pre_ffn=vec(g_pre_ffn), g_post_ffn=vec(g_post_ffn),
        merge_consts=(vec(g_pre_mix), vec(g_post_mix), vec(m_norm_g), conv_w[l], vec(b_gate),
                      cols(o_mo, o_if), cols(o_gate, o_gate + N_BRANCH * d),
                      (w_br_na, l), (w_br_m, l), (w_br_cv, l), (w_out, l)),
        wg=(w_ffn_gate, l), wu=(w_ffn_up, l), wd=(w_ffn_down, l),
    )


def _trunk_layer(x, mod, lw, batch, seq_len, ctx):
    is_ctx = ctx is None
    naqkv, mqv, mkt, ift, cvb, u = _inproj(x, mod, lw["g_pre_mix"], lw["wa"], lw["wkt"], lw["wift"], lw["bift"],
                                           lw["wcv"], seq_len=seq_len, na_dtype=F32 if is_ctx else BF16)
    if is_ctx:
        att = _context_attention(naqkv, batch, seq_len)
        c0 = jnp.zeros((batch, 2 * M_HEADS, M_DK, 2 * M_DV), F32)
        m0 = jnp.zeros((batch, 2 * M_HEADS, LANES), F32)
    else:
        k_ctx, v_ctx, bias, c0, m0 = ctx
        att = _neighborhood_attention(naqkv, k_ctx, v_ctx, bias, batch, seq_len)
    hf, hb, c_new, m_new = _bidir_mlstm(mqv, mkt, ift, c0, m0, batch, seq_len)
    consts = lw["merge_consts"] + (lw["g_pre_ffn"], lw["g_post_ffn"], lw["wg"], lw["wu"], lw["wd"])
    x = _merge(x, mod, att, hf, hb, cvb, u, consts, seq_len=seq_len)
    return x, (naqkv, c_new, m_new)


def kernel(x_prompt, x_sample, c, cache_k, cache_v, state_C, state_n, state_m, c_ctx, w_ada, b_ada, g_pre_mix,
           g_post_mix, g_pre_ffn, g_post_ffn, w_in, b_gate, m_b_i, m_b_f, na_rpb, m_norm_g, conv_w, w_br_na,
           w_br_m, w_br_cv, w_out, w_ffn_gate, w_ffn_up, w_ffn_down):
    batch, seq, d = x_prompt.shape
    dec_batch, dec_seq, _ = x_sample.shape
    depth = w_in.shape[0]
    past = cache_k.shape[3]
    assert dec_batch + 1 <= COND_ROWS

    cond = jnp.zeros((COND_ROWS, d), F32).at[:dec_batch].set(c).at[dec_batch].set(c_ctx)
    mod = _ada_modulation(cond, w_ada, b_ada).reshape(depth, COND_ROWS, N_MOD, d)

    xp = x_prompt.reshape(batch * seq, d)
    xs = x_sample.reshape(dec_batch * dec_seq, d)
    ks, vs, cs, ns, ms = [], [], [], [], []
    w_br_na, w_br_m, w_br_cv, w_out, w_ffn_gate, w_ffn_up, w_ffn_down = (
        w.astype(BF16) for w in (w_br_na, w_br_m, w_br_cv, w_out, w_ffn_gate, w_ffn_up, w_ffn_down))
    for l in range(depth):
        lw = _layer_weights(l, w_in, b_gate, m_b_i, m_b_f, g_pre_mix, g_post_mix, g_pre_ffn, g_post_ffn,
                            m_norm_g, conv_w, w_br_na, w_br_m, w_br_cv, w_out, w_ffn_gate, w_ffn_up, w_ffn_down)
        xp, (naqkv, c_l, m_l) = _trunk_layer(xp, mod[l, dec_batch:dec_batch + 1], lw, batch, seq, None)
        heads = lambda a: a.reshape(batch, seq, NA_HEADS, NA_HEAD_DIM).transpose(0, 2, 1, 3)
        ks.append(heads(naqkv[:, NA_WIDTH:2 * NA_WIDTH]))
        vs.append(heads(naqkv[:, 2 * NA_WIDTH:]))
        cs.append(c_l[..., :M_DV].reshape(batch, 2, M_HEADS, M_DK, M_DV))
        ns.append(c_l[..., M_DV].reshape(batch, 2, M_HEADS, M_DK))
        ms.append(m_l[:, :, 0].reshape(batch, 2, M_HEADS))

        tokens = lambda a: a.transpose(0, 2, 1, 3).reshape(dec_batch, past, NA_WIDTH).astype(BF16)
        c0 = jnp.concatenate([state_C[:, l], jnp.broadcast_to(state_n[:, l][..., None], state_C[:, l].shape)],
                             axis=-1).reshape(dec_batch, 2 * M_HEADS, M_DK, 2 * M_DV)
        m0 = jnp.broadcast_to(state_m[:, l].reshape(dec_batch, 2 * M_HEADS, 1), (dec_batch, 2 * M_HEADS, LANES))
        ctx = (tokens(cache_k[:, l]), tokens(cache_v[:, l]), _na_bias_tables(na_rpb[l], dec_seq // GRID_W), c0, m0)
        xs, _ = _trunk_layer(xs, mod[l, :dec_batch], lw, dec_batch, dec_seq, ctx)

    return (xp.reshape(batch, seq, d), xs.reshape(dec_batch, dec_seq, d),
            jnp.stack(ks, axis=1), jnp.stack(vs, axis=1), jnp.stack(cs, axis=1),
            jnp.stack(ns, axis=1), jnp.stack(ms, axis=1))
```
